```python
import functools
import jax, jax.numpy as jnp
from jax import lax
import numpy as np

D_MODEL = 4096
BATCH = 4
SEQ = 2048
DEPTH = 1
DEC_BATCH = 128
DEC_SEQ = 8
PAST_LEN = 16384
PAGE_SIZE = 128

N_HEADS = 16
QK_NOPE_DIM = 128
QK_ROPE_DIM = 64
V_HEAD_DIM = 128
Q_LORA_RANK = 1024
KV_LORA_RANK = 512
ATTN_WIDTH = N_HEADS * V_HEAD_DIM
CONV_WIDTH = D_MODEL - ATTN_WIDTH
CONV_KSIZE = 3
N_IN = Q_LORA_RANK + KV_LORA_RANK + QK_ROPE_DIM + 3 * CONV_WIDTH
D_FF = -(-8 * D_MODEL // (3 * 256)) * 256
ROPE_BASE = 10000.0
EPS = 1e-6
Q_BLOCK = 128
SOFTMAX_SCALE = (QK_NOPE_DIM + QK_ROPE_DIM) ** -0.5

kernel_name = "hymba_mla_shortconv_decoder_step"


def rmsnorm(x, g):
    xf = x.astype(jnp.float32)
    y = xf * lax.rsqrt(jnp.mean(xf * xf, axis=-1, keepdims=True) + EPS)
    return (y * g.astype(jnp.float32)).astype(x.dtype)


def rope(x, pos):
    r = x.shape[-1]
    inv_freq = ROPE_BASE ** (-jnp.arange(0, r, 2, dtype=jnp.float32) / r)
    ang = pos[:, None] * inv_freq[None, :]
    shape = (pos.shape[0],) + (1,) * (x.ndim - 3) + (r // 2,)
    cos = jnp.cos(ang).reshape(shape)
    sin = jnp.sin(ang).reshape(shape)
    xf = x.astype(jnp.float32)
    x1, x2 = xf[..., : r // 2], xf[..., r // 2:]
    return jnp.concatenate([x1 * cos - x2 * sin, x1 * sin + x2 * cos], axis=-1).astype(x.dtype)


def project(h, pos, w_in, g_q, w_uq, g_kv, w_uk):
    b, t, _ = h.shape
    z = h @ w_in
    o1 = Q_LORA_RANK
    o2 = o1 + KV_LORA_RANK
    o3 = o2 + QK_ROPE_DIM
    o4 = o3 + CONV_WIDTH
    o5 = o4 + CONV_WIDTH
    cq, ckv, kr = z[..., :o1], z[..., o1:o2], z[..., o2:o3]
    gb, gc, u = z[..., o3:o4], z[..., o4:o5], z[..., o5:]
    q = (rmsnorm(cq, g_q) @ w_uq).reshape(b, t, N_HEADS, QK_NOPE_DIM + QK_ROPE_DIM)
    q_nope, q_rope = q[..., :QK_NOPE_DIM], q[..., QK_NOPE_DIM:]
    q_lat = jnp.einsum('bthd,chd->bthc', q_nope, w_uk)
    q_rope = rope(q_rope, pos)
    kr = rope(kr, pos)
    ckv = rmsnorm(ckv, g_kv)
    return q_lat, q_rope, ckv, kr, gb, gc * u


def prompt_attention(q_lat, q_rope, ckv, kr):
    b, s, h, c = q_lat.shape
    nqb = s // Q_BLOCK
    qlb = q_lat.reshape(b, nqb, Q_BLOCK, h, c).swapaxes(0, 1)
    qrb = q_rope.reshape(b, nqb, Q_BLOCK, h, QK_ROPE_DIM).swapaxes(0, 1)
    kpos = jnp.arange(s)
    ckv_f = ckv.astype(jnp.float32)

    def block(args):
        ql, qr, i = args
        sc = (jnp.einsum('bqhc,bkc->bhqk', ql, ckv)
              + jnp.einsum('bqhr,bkr->bhqk', qr, kr)).astype(jnp.float32) * SOFTMAX_SCALE
        qpos = i * Q_BLOCK + jnp.arange(Q_BLOCK)
        sc = jnp.where(kpos[None, :] <= qpos[:, None], sc, -jnp.inf)
        p = jax.nn.softmax(sc, axis=-1)
        return jnp.einsum('bhqk,bkc->bqhc', p, ckv_f)

    o = lax.map(block, (qlb, qrb, jnp.arange(nqb)))
    return o.swapaxes(0, 1).reshape(b, s, h, c)


def sample_attention(q_lat, q_rope, ckv_new, kr_new, cache_ckv, cache_kr, page_table):
    t = q_lat.shape[1]
    s_new = (jnp.einsum('bthc,buc->bthu', q_lat, ckv_new)
             + jnp.einsum('bthr,bur->bthu', q_rope, kr_new)).astype(jnp.float32) * SOFTMAX_SCALE
    causal = jnp.arange(t)[None, :] <= jnp.arange(t)[:, None]
    s_new = jnp.where(causal[:, None, :], s_new, -jnp.inf)
    m = jnp.max(s_new, axis=-1)
    p = jnp.exp(s_new - m[..., None])
    l = jnp.sum(p, axis=-1)
    acc = jnp.einsum('bthu,buc->bthc', p, ckv_new.astype(jnp.float32))

    def step(carry, pages):
        m, l, acc = carry
        ck = cache_ckv[pages]
        kk = cache_kr[pages]
        sc = (jnp.einsum('bthc,bpc->bthp', q_lat, ck)
              + jnp.einsum('bthr,bpr->bthp', q_rope, kk)).astype(jnp.float32) * SOFTMAX_SCALE
        m_new = jnp.maximum(m, jnp.max(sc, axis=-1))
        corr = jnp.exp(m - m_new)
        pp = jnp.exp(sc - m_new[..., None])
        l = l * corr + jnp.sum(pp, axis=-1)
        acc = acc * corr[..., None] + jnp.einsum('bthp,bpc->bthc', pp, ck.astype(jnp.float32))
        return (m_new, l, acc), None

    (m, l, acc), _ = lax.scan(step, (m, l, acc), page_table.T)
    return acc / l[..., None]


def short_conv(v_hist, w_conv):
    t = v_hist.shape[1] - (CONV_KSIZE - 1)
    return sum(w_conv[k] * v_hist[:, k:k + t] for k in range(CONV_KSIZE))


def block_forward(x, pos, conv_prev, attend, g_attn, w_in, g_q, w_uq, g_kv, w_uk, w_uv,
                  w_conv, w_o, g_ffn, w_gate, w_up, w_down):
    b, t, _ = x.shape
    h = rmsnorm(x, g_attn)
    q_lat, q_rope, ckv, kr, gb, v = project(h, pos, w_in, g_q, w_uq, g_kv, w_uk)
    o_lat = attend(q_lat, q_rope, ckv, kr).astype(x.dtype)
    o_attn = jnp.einsum('bthc,chd->bthd', o_lat, w_uv).reshape(b, t, ATTN_WIDTH)
    v_hist = jnp.concatenate([conv_prev, v], axis=1)
    o_conv = gb * short_conv(v_hist, w_conv)
    x = x + jnp.concatenate([o_attn, o_conv], axis=-1) @ w_o
    h2 = rmsnorm(x, g_ffn)
    x = x + (jax.nn.silu(h2 @ w_gate) * (h2 @ w_up)) @ w_down
    return x, ckv, kr, v_hist[:, v_hist.shape[1] - (CONV_KSIZE - 1):]


def setup_inputs(seed: int = 0) -> dict:
    key = jax.random.key(seed)
    ks = jax.random.split(key, 24)
    n_pages = PAST_LEN // PAGE_SIZE
    n_used = DEC_BATCH * n_pages
    n_phys = (n_used * 5) // 4
    f32 = jnp.float32

    def nrm(k, shape, scale):
        return jax.random.normal(k, shape, f32) * scale

    def gain(k, shape):
        return 1.0 + 0.01 * jax.random.normal(k, shape, f32)

    page_table = jax.random.permutation(ks[0], n_phys)[:n_used].reshape(DEC_BATCH, n_pages).astype(jnp.int32)
    return {
        "x_prompt": nrm(ks[1], (BATCH, SEQ, D_MODEL), 1.0),
        "x_sample": nrm(ks[2], (DEC_BATCH, DEC_SEQ, D_MODEL), 1.0),
        "cache_ckv": nrm(ks[3], (DEPTH, n_phys, PAGE_SIZE, KV_LORA_RANK), 1.0),
        "cache_krope": nrm(ks[4], (DEPTH, n_phys, PAGE_SIZE, QK_ROPE_DIM), 1.0),
        "state_conv": nrm(ks[5], (DEPTH, DEC_BATCH, CONV_KSIZE - 1, CONV_WIDTH), 1.0),
        "page_table": page_table,
        "g_attn": gain(ks[6], (DEPTH, D_MODEL)),
        "w_in": nrm(ks[7], (DEPTH, D_MODEL, N_IN), D_MODEL ** -0.5),
        "g_q": gain(ks[8], (DEPTH, Q_LORA_RANK)),
        "w_uq": nrm(ks[9], (DEPTH, Q_LORA_RANK, N_HEADS * (QK_NOPE_DIM + QK_ROPE_DIM)), Q_LORA_RANK ** -0.5),
        "g_kv": gain(ks[10], (DEPTH, KV_LORA_RANK)),
        "w_uk": nrm(ks[11], (DEPTH, KV_LORA_RANK, N_HEADS, QK_NOPE_DIM), KV_LORA_RANK ** -0.5),
        "w_uv": nrm(ks[12], (DEPTH, KV_LORA_RANK, N_HEADS, V_HEAD_DIM), KV_LORA_RANK ** -0.5),
        "w_conv": nrm(ks[13], (DEPTH, CONV_KSIZE, CONV_WIDTH), CONV_KSIZE ** -0.5),
        "w_o": nrm(ks[14], (DEPTH, D_MODEL, D_MODEL), D_MODEL ** -0.5),
        "g_ffn": gain(ks[15], (DEPTH, D_MODEL)),
        "w_gate": nrm(ks[16], (DEPTH, D_MODEL, D_FF), D_MODEL ** -0.5),
        "w_up": nrm(ks[17], (DEPTH, D_MODEL, D_FF), D_MODEL ** -0.5),
        "w_down": nrm(ks[18], (DEPTH, D_FF, D_MODEL), D_FF ** -0.5),
        "g_final": gain(ks[19], (D_MODEL,)),
    }


def reference(x_prompt, x_sample, cache_ckv, cache_krope, state_conv, page_table,
              g_attn, w_in, g_q, w_uq, g_kv, w_uk, w_uv, w_conv, w_o,
              g_ffn, w_gate, w_up, w_down, g_final):
    pos_prompt = jnp.arange(x_prompt.shape[1], dtype=jnp.float32)
    pos_sample = PAST_LEN + jnp.arange(x_sample.shape[1], dtype=jnp.float32)
    xp, xs = x_prompt, x_sample
    ckv_p, kr_p, cv_p, ckv_s, kr_s, cv_s = [], [], [], [], [], []
    for l in range(DEPTH):
        w = (g_attn[l], w_in[l], g_q[l], w_uq[l], g_kv[l], w_uk[l], w_uv[l],
             w_conv[l], w_o[l], g_ffn[l], w_gate[l], w_up[l], w_down[l])
        conv0 = jnp.zeros((xp.shape[0], CONV_KSIZE - 1, CONV_WIDTH), xp.dtype)
        xp, a, b_, c = block_forward(xp, pos_prompt, conv0, prompt_attention, *w)
        ckv_p.append(a); kr_p.append(b_); cv_p.append(c)
        attend_s = functools.partial(sample_attention, cache_ckv=cache_ckv[l],
                                     cache_kr=cache_krope[l], page_table=page_table)
        xs, a, b_, c = block_forward(xs, pos_sample, state_conv[l], attend_s, *w)
        ckv_s.append(a); kr_s.append(b_); cv_s.append(c)
    y_prompt = rmsnorm(xp, g_final)
    y_sample = rmsnorm(xs, g_final)
    return (y_prompt, y_sample,
            jnp.stack(ckv_p), jnp.stack(kr_p), jnp.stack(cv_p),
            jnp.stack(ckv_s), jnp.stack(kr_s), jnp.stack(cv_s))
```

```python
import functools

import jax
import jax.numpy as jnp
from jax import lax
from jax.experimental import pallas as pl
from jax.experimental.pallas import tpu as pltpu

EPS = 1e-6
ROPE_BASE = 10000.0

F32 = jnp.float32
BF16 = jnp.bfloat16

LANES_V7X = 128
SUBLANES_V7X = 8
VMEM_BYTES_V7X = 64 * 1024 * 1024
VMEM_REQUEST_CAP = VMEM_BYTES_V7X - 6 * 1024 * 1024


def _nbytes(shape, dtype):
    n = 1
    for s in shape:
        n *= s
    return n * jnp.dtype(dtype).itemsize


def _vmem_limit(blocks, scratch=(), temps=()):
    need = 2 * sum(_nbytes(s, d) for s, d in blocks)
    need += sum(_nbytes(s, d) for s, d in scratch)
    need += sum(_nbytes(s, d) for s, d in temps)
    need += 2 * 1024 * 1024
    return int(min(max(need, 16 * 1024 * 1024), VMEM_REQUEST_CAP))


def _pick(total, pref, multiple):
    best = None
    d = multiple
    while d <= min(total, pref):
        if total % d == 0:
            best = d
        d += multiple
    return best if best is not None else total


def _params(semantics, vmem):
    return pltpu.CompilerParams(dimension_semantics=semantics, vmem_limit_bytes=vmem)


def _dot(a, b):
    return jnp.dot(a, b, preferred_element_type=F32)


def _dot_nt(a, b):
    return lax.dot_general(a, b, (((1,), (1,)), ((), ())), preferred_element_type=F32)


def _rms(x, g):
    return x * lax.rsqrt(jnp.mean(x * x, axis=-1, keepdims=True) + EPS) * g


def _rope(x, cos2, sin2):
    half = x.shape[-1] // 2
    swapped = jnp.concatenate([x[:, half:], x[:, :half]], axis=-1)
    return x * cos2 + swapped * sin2


def _rmsnorm_kernel(x_ref, g_ref, o_ref):
    o_ref[...] = _rms(x_ref[...].astype(F32), g_ref[...]).astype(o_ref.dtype)


def _rmsnorm(x, g, out_dtype):
    t, d = x.shape
    tr = _pick(t, 256, SUBLANES_V7X)
    return pl.pallas_call(
        _rmsnorm_kernel,
        out_shape=jax.ShapeDtypeStruct((t, d), out_dtype),
        grid=(t // tr,),
        in_specs=[pl.BlockSpec((tr, d), lambda i: (i, 0)),
                  pl.BlockSpec((1, d), lambda i: (0, 0))],
        out_specs=pl.BlockSpec((tr, d), lambda i: (i, 0)),
        compiler_params=_params(("parallel",), _vmem_limit(
            [((tr, d), x.dtype), ((tr, d), out_dtype)], temps=[((tr, d), F32)] * 2)),
        name="rmsnorm",
    )(x, g.reshape(1, d).astype(F32))


def _mm_kernel(*refs, n_pairs, has_res):
    a_refs = refs[:n_pairs]
    w_refs = refs[n_pairs:2 * n_pairs]
    o_ref = refs[-1]
    acc = _dot(a_refs[0][...], w_refs[0][...])
    for a_ref, w_ref in zip(a_refs[1:], w_refs[1:]):
        acc = acc + _dot(a_ref[...], w_ref[...])
    if has_res:
        acc = acc + refs[2 * n_pairs][...]
    o_ref[...] = acc.astype(o_ref.dtype)


def _mm(a_list, w_list, res, out_dtype, tm_pref, tn_pref, name):
    t = a_list[0][0].shape[0]
    n = w_list[0][0].shape[1]
    tm = _pick(t, tm_pref, 16)
    tn = _pick(n, tn_pref, LANES_V7X)
    in_specs, args, blocks = [], [], []
    for a, cb, kd in a_list:
        in_specs.append(pl.BlockSpec((tm, kd), lambda i, j, cb=cb: (i, cb)))
        args.append(a)
        blocks.append(((tm, kd), a.dtype))
    for w, rb, kd in w_list:
        in_specs.append(pl.BlockSpec((kd, tn), lambda i, j, rb=rb: (rb, j)))
        args.append(w)
        blocks.append(((kd, tn), w.dtype))
    if res is not None:
        in_specs.append(pl.BlockSpec((tm, tn), lambda i, j: (i, j)))
        args.append(res)
        blocks.append(((tm, tn), res.dtype))
    blocks.append(((tm, tn), out_dtype))
    return pl.pallas_call(
        functools.partial(_mm_kernel, n_pairs=len(a_list), has_res=res is not None),
        out_shape=jax.ShapeDtypeStruct((t, n), out_dtype),
        grid=(t // tm, n // tn),
        in_specs=in_specs,
        out_specs=pl.BlockSpec((tm, tn), lambda i, j: (i, j)),
        compiler_params=_params(("parallel", "arbitrary"),
                                _vmem_limit(blocks, temps=[((tm, tn), F32)] * 2)),
        name=name,
    )(*args)


def _qpath_kernel(cq_ref, g_ref, wn_ref, wr_ref, wuk_ref, cos_ref, sin_ref, ql_ref, qr_ref,
                  *, dn, dr):
    y = _rms(cq_ref[...], g_ref[...]).astype(BF16)
    qn = _dot(y, wn_ref[...])
    qr = _dot(y, wr_ref[...])
    cos2 = cos_ref[...]
    sin2 = sin_ref[...]
    for h in range(ql_ref.shape[0]):
        qn_h = qn[:, h * dn:(h + 1) * dn].astype(BF16)
        ql_ref[h] = _dot(qn_h, wuk_ref[h]).astype(ql_ref.dtype)
        qr_ref[h] = _rope(qr[:, h * dr:(h + 1) * dr], cos2, sin2).astype(qr_ref.dtype)


def _qpath(z_mla, g_q, w_nope, w_rope, w_ukt, cos2, sin2, out_dtype, tm_pref):
    t = z_mla.shape[0]
    rq = g_q.shape[0]
    n_heads, dn, c = w_ukt.shape
    dr = w_rope.shape[1] // n_heads
    tm = _pick(t, tm_pref, 16)
    blocks = [((tm, rq), F32), (w_nope.shape, BF16), (w_rope.shape, BF16), (w_ukt.shape, BF16),
              ((tm, LANES_V7X), F32), ((tm, LANES_V7X), F32),
              ((n_heads, tm, c), out_dtype), ((n_heads, tm, LANES_V7X), out_dtype)]
    return pl.pallas_call(
        functools.partial(_qpath_kernel, dn=dn, dr=dr),
        out_shape=(jax.ShapeDtypeStruct((n_heads, t, c), out_dtype),
                   jax.ShapeDtypeStruct((n_heads, t, dr), out_dtype)),
        grid=(t // tm,),
        in_specs=[pl.BlockSpec((tm, rq), lambda i: (i, 0)),
                  pl.BlockSpec((1, rq), lambda i: (0, 0)),
                  pl.BlockSpec(w_nope.shape, lambda i: (0, 0)),
                  pl.BlockSpec(w_rope.shape, lambda i: (0, 0)),
                  pl.BlockSpec(w_ukt.shape, lambda i: (0, 0, 0)),
                  pl.BlockSpec((tm, dr), lambda i: (i, 0)),
                  pl.BlockSpec((tm, dr), lambda i: (i, 0))],
        out_specs=(pl.BlockSpec((n_heads, tm, c), lambda i: (0, i, 0)),
                   pl.BlockSpec((n_heads, tm, dr), lambda i: (0, i, 0))),
        compiler_params=_params(("parallel",), _vmem_limit(
            blocks, temps=[((tm, w_nope.shape[1]), F32), ((tm, w_rope.shape[1]), F32),
                           ((tm, c), F32)])),
        name="qpath",
    )(z_mla, g_q.reshape(1, rq).astype(F32), w_nope, w_rope, w_ukt, cos2, sin2)


def _kvpath_kernel(ckv_ref, kr_ref, g_ref, cos_ref, sin_ref, c32_ref, c16_ref, k32_ref, k16_ref):
    ckv = _rms(ckv_ref[...], g_ref[...])
    c32_ref[...] = ckv
    c16_ref[...] = ckv.astype(BF16)
    dr = k32_ref.shape[-1]
    kr = _rope(kr_ref[:, :dr], cos_ref[...], sin_ref[...])
    k32_ref[...] = kr
    k16_ref[...] = kr.astype(BF16)


def _kvpath(z_mla, g_kv, cos2, sin2, rq, tm_pref):
    t = z_mla.shape[0]
    c = g_kv.shape[0]
    dr = cos2.shape[1]
    tm = _pick(t, tm_pref, 16)
    blocks = [((tm, c), F32), ((tm, LANES_V7X), F32), ((tm, LANES_V7X), F32) , ((tm, LANES_V7X), F32),
              ((tm, c), F32), ((tm, c), BF16), ((tm, LANES_V7X), F32), ((tm, LANES_V7X), BF16)]
    return pl.pallas_call(
        _kvpath_kernel,
        out_shape=(jax.ShapeDtypeStruct((t, c), F32), jax.ShapeDtypeStruct((t, c), BF16),
                   jax.ShapeDtypeStruct((t, dr), F32), jax.ShapeDtypeStruct((t, dr), BF16)),
        grid=(t // tm,),
        in_specs=[pl.BlockSpec((tm, c), lambda i: (i, rq // c)),
                  pl.BlockSpec((tm, LANES_V7X), lambda i: (i, (rq + c) // LANES_V7X)),
                  pl.BlockSpec((1, c), lambda i: (0, 0)),
                  pl.BlockSpec((tm, dr), lambda i: (i, 0)),
                  pl.BlockSpec((tm, dr), lambda i: (i, 0))],
        out_specs=(pl.BlockSpec((tm, c), lambda i: (i, 0)), pl.BlockSpec((tm, c), lambda i: (i, 0)),
                   pl.BlockSpec((tm, dr), lambda i: (i, 0)), pl.BlockSpec((tm, dr), lambda i: (i, 0))),
        compiler_params=_params(("parallel",), _vmem_limit(blocks, temps=[((tm, c), F32)] * 2)),
        name="kvpath",
    )(z_mla, z_mla, g_kv.reshape(1, c).astype(F32), cos2, sin2)


def _conv_kernel(gb_ref, gc_ref, u_ref, gcp_ref, up_ref, cp_ref, w_ref, o_ref, nc_ref):
    i = pl.program_id(1)
    nb, tt, w = gc_ref.shape
    v = gc_ref[...] * u_ref[...]
    vp = gcp_ref[...] * up_ref[...]
    first = i == 0
    hist = cp_ref[...]
    h1 = jnp.where(first, hist[:, 1:2, :], vp[:, 7:8, :])
    h0 = jnp.where(first, hist[:, 0:1, :], vp[:, 6:7, :])
    v2 = v.reshape(nb * tt, w)
    r1 = pltpu.roll(v2, 1, axis=0).reshape(nb, tt, w)
    r2 = pltpu.roll(v2, 2, axis=0).reshape(nb, tt, w)
    tpos = lax.broadcasted_iota(jnp.int32, (nb, tt, w), 1)
    vm1 = jnp.where(tpos == 0, h1, r1)
    vm2 = jnp.where(tpos == 0, h0, jnp.where(tpos == 1, h1, r2))
    wc = w_ref[...]
    y = wc[0:1, :] * vm2 + wc[1:2, :] * vm1 + wc[2:3, :] * v
    o_ref[...] = (gb_ref[...] * y).astype(o_ref.dtype)

    @pl.when(i == pl.num_programs(1) - 1)
    def _():
        nc_ref[...] = v[:, tt - 2:tt, :]


def _conv(z_conv, conv_prev, w_conv, nb, tt):
    b, t, w3 = z_conv.shape
    w = w3 // 3
    assert w_conv.shape[0] == 3 and conv_prev.shape[1] == 2
    pb = tt // SUBLANES_V7X

    def prev_map(col):
        return lambda bi, i: (bi, jnp.maximum(i * pb - 1, 0), col)

    blocks = [((nb, tt, w), F32)] * 3 + [((nb, 8, w), F32)] * 4 + [((nb, tt, w), BF16)]
    return pl.pallas_call(
        _conv_kernel,
        out_shape=(jax.ShapeDtypeStruct((b, t, w), BF16), jax.ShapeDtypeStruct((b, 2, w), F32)),
        grid=(b // nb, t // tt),
        in_specs=[pl.BlockSpec((nb, tt, w), lambda bi, i: (bi, i, 0)),
                  pl.BlockSpec((nb, tt, w), lambda bi, i: (bi, i, 1)),
                  pl.BlockSpec((nb, tt, w), lambda bi, i: (bi, i, 2)),
                  pl.BlockSpec((nb, SUBLANES_V7X, w), prev_map(1)),
                  pl.BlockSpec((nb, SUBLANES_V7X, w), prev_map(2)),
                  pl.BlockSpec((nb, 2, w), lambda bi, i: (bi, 0, 0)),
                  pl.BlockSpec((3, w), lambda bi, i: (0, 0))],
        out_specs=(pl.BlockSpec((nb, tt, w), lambda bi, i: (bi, i, 0)),
                   pl.BlockSpec((nb, 2, w), lambda bi, i: (bi, 0, 0))),
        compiler_params=_params(("parallel", "arbitrary"),
                                _vmem_limit(blocks, temps=[((nb, tt, w), F32)] * 6)),
        name="conv",
    )(z_conv, z_conv, z_conv, z_conv, z_conv, conv_prev, w_conv)


def _pattn_kernel(ql_ref, qr_ref, k_ref, kr_ref, o_ref, m_sc, l_sc, acc_sc, *, bq, bk, scale):
    qi = pl.program_id(1)
    kj = pl.program_id(2)
    n_heads, _, c = ql_ref.shape
    rows = n_heads * bq
    last = ((qi + 1) * bq - 1) // bk

    @pl.when(kj == 0)
    def _():
        m_sc[...] = jnp.full(m_sc.shape, -jnp.inf, F32)
        l_sc[...] = jnp.zeros(l_sc.shape, F32)
        acc_sc[...] = jnp.zeros(acc_sc.shape, F32)

    def step(masked):
        q = ql_ref[...].reshape(rows, c)
        qr = qr_ref[...].reshape(rows, qr_ref.shape[-1])
        k = k_ref[...]
        s = (_dot_nt(q, k) + _dot_nt(qr, kr_ref[...])) * scale
        if masked:
            row = lax.broadcasted_iota(jnp.int32, (rows, bk), 0)
            qpos = qi * bq + row % bq
            kpos = kj * bk + lax.broadcasted_iota(jnp.int32, (rows, bk), 1)
            s = jnp.where(kpos <= qpos, s, -jnp.inf)
        m_prev = m_sc[...]
        m_new = jnp.maximum(m_prev, jnp.max(s, axis=-1, keepdims=True))
        corr = jnp.exp(m_prev - m_new)
        p = jnp.exp(s - m_new[:, :1])
        l_sc[...] = corr * l_sc[...] + jnp.sum(p, axis=-1, keepdims=True)
        acc_sc[...] = acc_sc[...] * corr[:, :1] + _dot(p.astype(BF16), k)
        m_sc[...] = m_new

    @pl.when(kj < last)
    def _():
        step(False)

    @pl.when(kj == last)
    def _():
        step(True)
        o = acc_sc[...] / l_sc[:, :1]
        o_ref[...] = o.reshape(n_heads, bq, c).astype(o_ref.dtype)


def _prompt_attention(ql, qr, k16, kr16, batch, seq, scale, bq_pref, bk_pref):
    n_heads, t, c = ql.shape
    dr = qr.shape[-1]
    bq = _pick(seq, bq_pref, 16)
    bk = _pick(seq, bk_pref, bq)
    assert bk % bq == 0
    nq, nk = seq // bq, seq // bk
    rows = n_heads * bq

    def kv_map(b, qi, kj):
        return (b * nk + jnp.minimum(kj, ((qi + 1) * bq - 1) // bk), 0)

    blocks = [((n_heads, bq, c), BF16), ((n_heads, bq, LANES_V7X), BF16), ((bk, c), BF16),
              ((bk, LANES_V7X), BF16), ((n_heads, bq, c), BF16)]
    scratch = [((rows, LANES_V7X), F32), ((rows, LANES_V7X), F32), ((rows, c), F32)]
    return pl.pallas_call(
        functools.partial(_pattn_kernel, bq=bq, bk=bk, scale=scale),
        out_shape=jax.ShapeDtypeStruct((n_heads, t, c), BF16),
        grid=(batch, nq, nk),
        in_specs=[pl.BlockSpec((n_heads, bq, c), lambda b, qi, kj: (0, b * nq + qi, 0)),
                  pl.BlockSpec((n_heads, bq, dr), lambda b, qi, kj: (0, b * nq + qi, 0)),
                  pl.BlockSpec((bk, c), kv_map),
                  pl.BlockSpec((bk, dr), kv_map)],
        out_specs=pl.BlockSpec((n_heads, bq, c), lambda b, qi, kj: (0, b * nq + qi, 0)),
        scratch_shapes=[pltpu.VMEM(s, d) for s, d in scratch],
        compiler_params=_params(("parallel", "parallel", "arbitrary"), _vmem_limit(
            blocks, scratch, temps=[((rows, bk), F32)] * 3 + [((rows, c), F32)] * 2)),
        name="prompt_attn",
    )(ql, qr, k16, kr16)


def _sattn_kernel(pt_ref, ql_ref, qr_ref, cn_ref, kn_ref, *rest, pages, scale):
    del pt_ref
    ck_refs = rest[:pages]
    kk_refs = rest[pages:2 * pages]
    o_ref, m_sc, l_sc, acc_sc = rest[2 * pages:]
    j = pl.program_id(1)
    n_heads, tn, c = ql_ref.shape
    rows = n_heads * tn
    q = ql_ref[...].reshape(rows, c)
    qr = qr_ref[...].reshape(rows, qr_ref.shape[-1])

    @pl.when(j == 0)
    def _():
        cn = cn_ref[...]
        s = (_dot_nt(q, cn) + _dot_nt(qr, kn_ref[...])) * scale
        tq = lax.broadcasted_iota(jnp.int32, (rows, tn), 0) % tn
        tk = lax.broadcasted_iota(jnp.int32, (rows, tn), 1)
        s = jnp.where(tk <= tq, s, -jnp.inf)
        m = jnp.max(s, axis=-1, keepdims=True)
        p = jnp.exp(s - m)
        m_sc[...] = jnp.broadcast_to(m, m_sc.shape)
        l_sc[...] = jnp.broadcast_to(jnp.sum(p, axis=-1, keepdims=True), l_sc.shape)
        acc_sc[...] = _dot(p, cn)

    s_pages = [(_dot_nt(q, ck[...]) + _dot_nt(qr, kk[...])) * scale
               for ck, kk in zip(ck_refs, kk_refs)]
    m_prev = m_sc[...]
    m_cur = s_pages[0].max(axis=-1, keepdims=True)
    for s in s_pages[1:]:
        m_cur = jnp.maximum(m_cur, s.max(axis=-1, keepdims=True))
    m_new = jnp.maximum(m_prev, m_cur)
    corr = jnp.exp(m_prev - m_new)
    l_new = corr * l_sc[...]
    acc = acc_sc[...] * corr[:, :1]
    for s, ck in zip(s_pages, ck_refs):
        p = jnp.exp(s - m_new[:, :1])
        l_new = l_new + jnp.sum(p, axis=-1, keepdims=True)
        acc = acc + _dot(p, ck[...])
    m_sc[...] = m_new
    l_sc[...] = l_new
    acc_sc[...] = acc

    @pl.when(j == pl.num_programs(1) - 1)
    def _():
        o_ref[...] = (acc / l_new[:, :1]).reshape(n_heads, tn, c)


def _sample_attention(ql, qr, ckv_new, kr_new, cache_ckv, cache_kr, page_table, scale, pages_pref):
    n_heads, t, c = ql.shape
    dr = qr.shape[-1]
    db, n_pages = page_table.shape
    tn = t // db
    page = cache_ckv.shape[1]
    pages = _pick(n_pages, pages_pref, 1)
    rows = n_heads * tn

    def page_map(r):
        return lambda b, j, pt: (pt[b * n_pages + j * pages + r], 0, 0)

    in_specs = [pl.BlockSpec((n_heads, tn, c), lambda b, j, pt: (0, b, 0)),
                pl.BlockSpec((n_heads, tn, dr), lambda b, j, pt: (0, b, 0)),
                pl.BlockSpec((tn, c), lambda b, j, pt: (b, 0)),
                pl.BlockSpec((tn, dr), lambda b, j, pt: (b, 0))]
    in_specs += [pl.BlockSpec((None, page, c), page_map(r)) for r in range(pages)]
    in_specs += [pl.BlockSpec((None, page, dr), page_map(r)) for r in range(pages)]
    blocks = [((n_heads, tn, c), F32), ((n_heads, tn, LANES_V7X), F32), ((tn, c), F32),
              ((tn, LANES_V7X), F32), ((n_heads, tn, c), F32)]
    blocks += [((page, c), F32), ((page, LANES_V7X), F32)] * pages
    scratch = [((rows, LANES_V7X), F32), ((rows, LANES_V7X), F32), ((rows, c), F32)]
    return pl.pallas_call(
        functools.partial(_sattn_kernel, pages=pages, scale=scale),
        out_shape=jax.ShapeDtypeStruct((n_heads, t, c), F32),
        grid_spec=pltpu.PrefetchScalarGridSpec(
            num_scalar_prefetch=1,
            grid=(db, n_pages // pages),
            in_specs=in_specs,
            out_specs=pl.BlockSpec((n_heads, tn, c), lambda b, j, pt: (0, b, 0)),
            scratch_shapes=[pltpu.VMEM(s, d) for s, d in scratch]),
        compiler_params=_params(("parallel", "arbitrary"), _vmem_limit(
            blocks, scratch, temps=[((rows, page), F32)] * (2 * pages) + [((rows, c), F32)] * 2)),
        name="sample_attn",
    )(page_table.reshape(-1), ql, qr, ckv_new, kr_new, *([cache_ckv] * pages), *([cache_kr] * pages))


def _uv_kernel(ol_ref, w_ref, o_ref):
    dv = w_ref.shape[-1]
    for h in range(ol_ref.shape[0]):
        o_ref[:, h * dv:(h + 1) * dv] = _dot(ol_ref[h].astype(BF16), w_ref[h]).astype(o_ref.dtype)


def _uv(o_lat, w_uvt, tm_pref):
    n_heads, t, c = o_lat.shape
    dv = w_uvt.shape[-1]
    tm = _pick(t, tm_pref, 16)
    blocks = [((n_heads, tm, c), o_lat.dtype), (w_uvt.shape, BF16), ((tm, n_heads * dv), BF16)]
    return pl.pallas_call(
        _uv_kernel,
        out_shape=jax.ShapeDtypeStruct((t, n_heads * dv), BF16),
        grid=(t // tm,),
        in_specs=[pl.BlockSpec((n_heads, tm, c), lambda i: (0, i, 0)),
                  pl.BlockSpec(w_uvt.shape, lambda i: (0, 0, 0))],
        out_specs=pl.BlockSpec((tm, n_heads * dv), lambda i: (i, 0)),
        compiler_params=_params(("parallel",), _vmem_limit(blocks, temps=[((tm, c), BF16)] * 2)),
        name="uv",
    )(o_lat, w_uvt)


def _gate_up_kernel(h_ref, wg_ref, wu_ref, o_ref):
    h = h_ref[...]
    g = _dot(h, wg_ref[...])
    u = _dot(h, wu_ref[...])
    o_ref[...] = (g * jax.nn.sigmoid(g) * u).astype(o_ref.dtype)


def _gate_up(h, w_gate, w_up, tm_pref, tn_pref):
    t, d = h.shape
    f = w_gate.shape[1]
    tm = _pick(t, tm_pref, 16)
    tn = _pick(f, tn_pref, LANES_V7X)
    blocks = [((tm, d), BF16), ((d, tn), BF16), ((d, tn), BF16), ((tm, tn), BF16)]
    return pl.pallas_call(
        _gate_up_kernel,
        out_shape=jax.ShapeDtypeStruct((t, f), BF16),
        grid=(t // tm, f // tn),
        in_specs=[pl.BlockSpec((tm, d), lambda i, j: (i, 0)),
                  pl.BlockSpec((d, tn), lambda i, j: (0, j)),
                  pl.BlockSpec((d, tn), lambda i, j: (0, j))],
        out_specs=pl.BlockSpec((tm, tn), lambda i, j: (i, j)),
        compiler_params=_params(("parallel", "arbitrary"),
                                _vmem_limit(blocks, temps=[((tm, tn), F32)] * 4)),
        name="gate_up",
    )(h, w_gate, w_up)


def _rope_tables(pos, dr):
    inv_freq = ROPE_BASE ** (-jnp.arange(0, dr, 2, dtype=F32) / dr)
    ang = pos[:, None] * inv_freq[None, :]
    cos, sin = jnp.cos(ang), jnp.sin(ang)
    return jnp.concatenate([cos, cos], axis=-1), jnp.concatenate([-sin, sin], axis=-1)


def _prep_weights(w_in, w_uq, w_uk, w_uv, w_o, w_gate, w_up, w_down, rq, c, dr):
    d = w_in.shape[0]
    n_heads, dn = w_uk.shape[1], w_uk.shape[2]
    n_mla = rq + c + dr
    pad = (-n_mla) % LANES_V7X
    w_mla = jnp.concatenate([w_in[:, :n_mla], jnp.zeros((d, pad), w_in.dtype)], axis=1).astype(BF16)
    w_cv = w_in[:, n_mla:].astype(BF16)
    w_uq3 = w_uq.reshape(rq, n_heads, dn + dr)
    w_nope = w_uq3[:, :, :dn].reshape(rq, n_heads * dn).astype(BF16)
    w_rope = w_uq3[:, :, dn:].reshape(rq, n_heads * dr).astype(BF16)
    w_ukt = jnp.transpose(w_uk, (1, 2, 0)).astype(BF16)
    w_uvt = jnp.transpose(w_uv, (1, 0, 2)).astype(BF16)
    return (w_mla, w_cv, w_nope, w_rope, w_ukt, w_uvt, w_o.astype(BF16), w_gate.astype(BF16),
            w_up.astype(BF16), w_down.astype(BF16))


def _block(x, pos, conv_prev, attend, is_prompt, g_attn, g_q, g_kv, w_conv, g_ffn, wts, scale):
    (w_mla, w_cv, w_nope, w_rope, w_ukt, w_uvt, w_o, w_gate, w_up, w_down) = wts
    b, t, d = x.shape
    bt = b * t
    rq, c = g_q.shape[0], g_kv.shape[0]
    dr = w_rope.shape[1] // w_ukt.shape[0]
    cw = w_conv.shape[1]
    x2 = x.reshape(bt, d)
    tm_big = 1024

    h = _rmsnorm(x2, g_attn, BF16)
    z_mla = _mm([(h, 0, d)], [(w_mla, 0, d)], None, F32, 512, w_mla.shape[1], "in_proj_mla")
    z_cv = _mm([(h, 0, d)], [(w_cv, 0, d)], None, F32, tm_big, 512, "in_proj_conv")

    cos2, sin2 = _rope_tables(pos, dr)
    cos2 = jnp.tile(cos2, (b, 1))
    sin2 = jnp.tile(sin2, (b, 1))
    ql, qr = _qpath(z_mla, g_q, w_nope, w_rope, w_ukt, cos2, sin2, BF16 if is_prompt else F32, 256)
    ckv32, ckv16, kr32, kr16 = _kvpath(z_mla, g_kv, cos2, sin2, rq, 512)

    if is_prompt:
        o_lat = _prompt_attention(ql, qr, ckv16, kr16, b, t, scale, 128, 512)
        o_conv, new_conv = _conv(z_cv.reshape(b, t, 3 * cw), conv_prev, w_conv, 1, _pick(t, 512, 8))
    else:
        o_lat = attend(ql, qr, ckv32, kr32)
        o_conv, new_conv = _conv(z_cv.reshape(b, t, 3 * cw), conv_prev, w_conv, _pick(b, 16, 1), t)
    o_attn = _uv(o_lat, w_uvt, 256)
    aw = o_attn.shape[1]
    x1 = _mm([(o_attn, 0, aw), (o_conv.reshape(bt, cw), 0, cw)],
             [(w_o, 0, aw), (w_o, aw // cw, cw)], x2, F32, tm_big, 512, "out_proj")
    h2 = _rmsnorm(x1, g_ffn, BF16)
    act = _gate_up(h2, w_gate, w_up, tm_big, 256)
    f = act.shape[1]
    x_out = _mm([(act, 0, f)], [(w_down, 0, f)], x1, F32, 512, 256, "down_proj")
    return x_out, ckv32, kr32, new_conv


def kernel(x_prompt, x_sample, cache_ckv, cache_krope, state_conv, page_table, g_attn, w_in, g_q,
           w_uq, g_kv, w_uk, w_uv, w_conv, w_o, g_ffn, w_gate, w_up, w_down, g_final):
    depth = w_in.shape[0]
    b, s, d = x_prompt.shape
    db, ts, _ = x_sample.shape
    rq, c = g_q.shape[1], g_kv.shape[1]
    dn = w_uk.shape[3]
    dr = w_uq.shape[2] // w_uk.shape[2] - dn
    cw = w_conv.shape[2]
    aw = w_uv.shape[2] * w_uv.shape[3]
    assert w_o.shape[1] == aw + cw and aw % cw == 0 and rq % c == 0
    past_len = page_table.shape[1] * cache_ckv.shape[2]
    scale = float(dn + dr) ** -0.5
    pos_p = jnp.arange(s, dtype=F32)
    pos_s = past_len + jnp.arange(ts, dtype=F32)

    xp, xs = x_prompt, x_sample
    outs = [[] for _ in range(6)]
    for l in range(depth):
        wts = _prep_weights(w_in[l], w_uq[l], w_uk[l], w_uv[l], w_o[l], w_gate[l], w_up[l],
                            w_down[l], rq, c, dr)
        small = (g_attn[l], g_q[l], g_kv[l], w_conv[l], g_ffn[l])
        conv0 = jnp.zeros((b, w_conv.shape[1] - 1, cw), F32)
        xp, ckv_p, kr_p, cv_p = _block(xp, pos_p, conv0, None, True, *small, wts, scale)
        xp = xp.reshape(b, s, d)
        attend = functools.partial(_sample_attention, cache_ckv=cache_ckv[l], cache_kr=cache_krope[l],
                                   page_table=page_table, scale=scale, pages_pref=8)
        xs, ckv_s, kr_s, cv_s = _block(xs, pos_s, state_conv[l], attend, False, *small, wts, scale)
        xs = xs.reshape(db, ts, d)
        for acc, val in zip(outs, (ckv_p.reshape(b, s, c), kr_p.reshape(b, s, dr), cv_p,
                                   ckv_s.reshape(db, ts, c), kr_s.reshape(db, ts, dr), cv_s)):
            acc.append(val)
    y_p = _rmsnorm(xp.reshape(b * s, d), g_final, F32).reshape(b, s, d)
    y_s = _rmsnorm(xs.reshape(db * ts, d), g_final, F32).reshape(db, ts, d)
    return (y_p, y_s) + tuple(jnp.stack(o) for o in outs)
```

```python
import functools

import jax
import jax.numpy as jnp
from jax import lax
from jax.experimental import pallas as pl
from jax.experimental.pallas import tpu as pltpu

EPS = 1e-6
ROPE_BASE = 10000.0

F32 = jnp.float32
BF16 = jnp.bfloat16

LANES_V7X = 128
SUBLANES_V7X = 8
VMEM_BYTES_V7X = 64 * 1024 * 1024
VMEM_REQUEST_CAP = VMEM_BYTES_V7X - 6 * 1024 * 1024


def _nbytes(shape, dtype):
    n = 1
    for s in shape:
        n *= s
    return n * jnp.dtype(dtype).itemsize


def _vmem_limit(blocks, scratch=(), temps=()):
    need = 2 * sum(_nbytes(s, d) for s, d in blocks)
    need += sum(_nbytes(s, d) for s, d in scratch)
    need += sum(_nbytes(s, d) for s, d in temps)
    need += 2 * 1024 * 1024
    return int(min(max(need, 16 * 1024 * 1024), VMEM_REQUEST_CAP))


def _pick(total, pref, multiple):
    best = None
    d = multiple
    while d <= min(total, pref):
        if total % d == 0:
            best = d
        d += multiple
    return best if best is not None else total


def _params(semantics, vmem):
    return pltpu.CompilerParams(dimension_semantics=semantics, vmem_limit_bytes=vmem)


def _dot(a, b):
    return jnp.dot(a, b, preferred_element_type=F32)


def _dot_nt(a, b):
    return lax.dot_general(a, b, (((1,), (1,)), ((), ())), preferred_element_type=F32)


def _rms(x, g):
    return x * lax.rsqrt(jnp.mean(x * x, axis=-1, keepdims=True) + EPS) * g


def _rope(x, cos2, sin2):
    half = x.shape[-1] // 2
    swapped = jnp.concatenate([x[:, half:], x[:, :half]], axis=-1)
    return x * cos2 + swapped * sin2


def _rmsnorm_kernel(x_ref, g_ref, o_ref):
    o_ref[...] = _rms(x_ref[...].astype(F32), g_ref[...]).astype(o_ref.dtype)


def _rmsnorm(x, g, out_dtype):
    t, d = x.shape
    tr = _pick(t, 256, SUBLANES_V7X)
    return pl.pallas_call(
        _rmsnorm_kernel,
        out_shape=jax.ShapeDtypeStruct((t, d), out_dtype),
        grid=(t // tr,),
        in_specs=[pl.BlockSpec((tr, d), lambda i: (i, 0)),
                  pl.BlockSpec((1, d), lambda i: (0, 0))],
        out_specs=pl.BlockSpec((tr, d), lambda i: (i, 0)),
        compiler_params=_params(("parallel",), _vmem_limit(
            [((tr, d), x.dtype), ((tr, d), out_dtype)], temps=[((tr, d), F32)] * 2)),
        name="rmsnorm",
    )(x, g.reshape(1, d).astype(F32))


def _mm_kernel(*refs, n_pairs, has_res):
    a_refs = refs[:n_pairs]
    w_refs = refs[n_pairs:2 * n_pairs]
    o_ref = refs[-1]
    acc = _dot(a_refs[0][...], w_refs[0][...])
    for a_ref, w_ref in zip(a_refs[1:], w_refs[1:]):
        acc = acc + _dot(a_ref[...], w_ref[...])
    if has_res:
        acc = acc + refs[2 * n_pairs][...]
    o_ref[...] = acc.astype(o_ref.dtype)


def _mm(a_list, w_list, res, out_dtype, tm_pref, tn_pref, name):
    t = a_list[0][0].shape[0]
    n = w_list[0][0].shape[1]
    tm = _pick(t, tm_pref, 16)
    tn = _pick(n, tn_pref, LANES_V7X)
    in_specs, args, blocks = [], [], []
    for a, cb, kd in a_list:
        in_specs.append(pl.BlockSpec((tm, kd), lambda i, j, cb=cb: (i, cb)))
        args.append(a)
        blocks.append(((tm, kd), a.dtype))
    for w, rb, kd in w_list:
        in_specs.append(pl.BlockSpec((kd, tn), lambda i, j, rb=rb: (rb, j)))
        args.append(w)
        blocks.append(((kd, tn), w.dtype))
    if res is not None:
        in_specs.append(pl.BlockSpec((tm, tn), lambda i, j: (i, j)))
        args.append(res)
        blocks.append(((tm, tn), res.dtype))
    blocks.append(((tm, tn), out_dtype))
    return pl.pallas_call(
        functools.partial(_mm_kernel, n_pairs=len(a_list), has_res=res is not None),
        out_shape=jax.ShapeDtypeStruct((t, n), out_dtype),
        grid=(t // tm, n // tn),
        in_specs=in_specs,
        out_specs=pl.BlockSpec((tm, tn), lambda i, j: (i, j)),
        compiler_params=_params(("parallel", "arbitrary"),
                                _vmem_limit(blocks, temps=[((tm, tn), F32)] * 2)),
        name=name,
    )(*args)


def _qpath_kernel(cq_ref, g_ref, wn_ref, wr_ref, wuk_ref, cos_ref, sin_ref, ql_ref, qr_ref,
                  *, dn, dr):
    y = _rms(cq_ref[...], g_ref[...]).astype(BF16)
    qn = _dot(y, wn_ref[...])
    qr = _dot(y, wr_ref[...])
    cos2 = cos_ref[...]
    sin2 = sin_ref[...]
    for h in range(ql_ref.shape[0]):
        qn_h = qn[:, h * dn:(h + 1) * dn].astype(BF16)
        ql_ref[h] = _dot(qn_h, wuk_ref[h]).astype(ql_ref.dtype)
        qr_ref[h] = _rope(qr[:, h * dr:(h + 1) * dr], cos2, sin2).astype(qr_ref.dtype)


def _qpath(z_mla, g_q, w_nope, w_rope, w_ukt, cos2, sin2, out_dtype, tm_pref):
    t = z_mla.shape[0]
    rq = g_q.shape[0]
    n_heads, dn, c = w_ukt.shape
    dr = w_rope.shape[1] // n_heads
    tm = _pick(t, tm_pref, 16)
    blocks = [((tm, rq), F32), (w_nope.shape, BF16), (w_rope.shape, BF16), (w_ukt.shape, BF16),
              ((tm, LANES_V7X), F32), ((tm, LANES_V7X), F32),
              ((n_heads, tm, c), out_dtype), ((n_heads, tm, LANES_V7X), out_dtype)]
    return pl.pallas_call(
        functools.partial(_qpath_kernel, dn=dn, dr=dr),
        out_shape=(jax.ShapeDtypeStruct((n_heads, t, c), out_dtype),
                   jax.ShapeDtypeStruct((n_heads, t, dr), out_dtype)),
        grid=(t // tm,),
        in_specs=[pl.BlockSpec((tm, rq), lambda i: (i, 0)),
                  pl.BlockSpec((1, rq), lambda i: (0, 0)),
                  pl.BlockSpec(w_nope.shape, lambda i: (0, 0)),
                  pl.BlockSpec(w_rope.shape, lambda i: (0, 0)),
                  pl.BlockSpec(w_ukt.shape, lambda i: (0, 0, 0)),
                  pl.BlockSpec((tm, dr), lambda i: (i, 0)),
                  pl.BlockSpec((tm, dr), lambda i: (i, 0))],
        out_specs=(pl.BlockSpec((n_heads, tm, c), lambda i: (0, i, 0)),
                   pl.BlockSpec((n_heads, tm, dr), lambda i: (0, i, 0))),
        compiler_params=_params(("parallel",), _vmem_limit(
            blocks, temps=[((tm, w_nope.shape[1]), F32), ((tm, w_rope.shape[1]), F32),
                           ((tm, c), F32)])),
        name="qpath",
    )(z_mla, g_q.reshape(1, rq).astype(F32), w_nope, w_rope, w_ukt, cos2, sin2)


def _kvpath_kernel(ckv_ref, kr_ref, g_ref, cos_ref, sin_ref, c32_ref, c16_ref, k32_ref, k16_ref):
    ckv = _rms(ckv_ref[...], g_ref[...])
    c32_ref[...] = ckv
    c16_ref[...] = ckv.astype(BF16)
    dr = k32_ref.shape[-1]
    kr = _rope(kr_ref[:, :dr], cos_ref[...], sin_ref[...])
    k32_ref[...] = kr
    k16_ref[...] = kr.astype(BF16)


def _kvpath(z_mla, g_kv, cos2, sin2, rq, tm_pref):
    t = z_mla.shape[0]
    c = g_kv.shape[0]
    dr = cos2.shape[1]
    tm = _pick(t, tm_pref, 16)
    blocks = [((tm, c), F32), ((tm, LANES_V7X), F32), ((tm, LANES_V7X), F32) , ((tm, LANES_V7X), F32),
              ((tm, c), F32), ((tm, c), BF16), ((tm, LANES_V7X), F32), ((tm, LANES_V7X), BF16)]
    return pl.pallas_call(
        _kvpath_kernel,
        out_shape=(jax.ShapeDtypeStruct((t, c), F32), jax.ShapeDtypeStruct((t, c), BF16),
                   jax.ShapeDtypeStruct((t, dr), F32), jax.ShapeDtypeStruct((t, dr), BF16)),
        grid=(t // tm,),
        in_specs=[pl.BlockSpec((tm, c), lambda i: (i, rq // c)),
                  pl.BlockSpec((tm, LANES_V7X), lambda i: (i, (rq + c) // LANES_V7X)),
                  pl.BlockSpec((1, c), lambda i: (0, 0)),
                  pl.BlockSpec((tm, dr), lambda i: (i, 0)),
                  pl.BlockSpec((tm, dr), lambda i: (i, 0))],
        out_specs=(pl.BlockSpec((tm, c), lambda i: (i, 0)), pl.BlockSpec((tm, c), lambda i: (i, 0)),
                   pl.BlockSpec((tm, dr), lambda i: (i, 0)), pl.BlockSpec((tm, dr), lambda i: (i, 0))),
        compiler_params=_params(("parallel",), _vmem_limit(blocks, temps=[((tm, c), F32)] * 2)),
        name="kvpath",
    )(z_mla, z_mla, g_kv.reshape(1, c).astype(F32), cos2, sin2)


def _conv_kernel(gb_ref, gc_ref, u_ref, gcp_ref, up_ref, cp_ref, w_ref, o_ref, nc_ref):
    i = pl.program_id(1)
    nb, tt, w = gc_ref.shape
    v = gc_ref[...] * u_ref[...]
    vp = gcp_ref[...] * up_ref[...]
    first = i == 0
    hist = cp_ref[...]
    h1 = jnp.where(first, hist[:, 1:2, :], vp[:, 7:8, :])
    h0 = jnp.where(first, hist[:, 0:1, :], vp[:, 6:7, :])
    v2 = v.reshape(nb * tt, w)
    r1 = pltpu.roll(v2, 1, axis=0).reshape(nb, tt, w)
    r2 = pltpu.roll(v2, 2, axis=0).reshape(nb, tt, w)
    tpos = lax.broadcasted_iota(jnp.int32, (nb, tt, w), 1)
    vm1 = jnp.where(tpos == 0, h1, r1)
    vm2 = jnp.where(tpos == 0, h0, jnp.where(tpos == 1, h1, r2))
    wc = w_ref[...]
    y = wc[0:1, :] * vm2 + wc[1:2, :] * vm1 + wc[2:3, :] * v
    o_ref[...] = (gb_ref[...] * y).astype(o_ref.dtype)

    @pl.when(i == pl.num_programs(1) - 1)
    def _():
        nc_ref[...] = v[:, tt - 2:tt, :]


def _conv(z_conv, conv_prev, w_conv, nb, tt):
    b, t, w3 = z_conv.shape
    w = w3 // 3
    assert w_conv.shape[0] == 3 and conv_prev.shape[1] == 2
    pb = tt // SUBLANES_V7X

    def prev_map(col):
        return lambda bi, i: (bi, jnp.maximum(i * pb - 1, 0), col)

    blocks = [((nb, tt, w), F32)] * 3 + [((nb, 8, w), F32)] * 4 + [((nb, tt, w), BF16)]
    return pl.pallas_call(
        _conv_kernel,
        out_shape=(jax.ShapeDtypeStruct((b, t, w), BF16), jax.ShapeDtypeStruct((b, 2, w), F32)),
        grid=(b // nb, t // tt),
        in_specs=[pl.BlockSpec((nb, tt, w), lambda bi, i: (bi, i, 0)),
                  pl.BlockSpec((nb, tt, w), lambda bi, i: (bi, i, 1)),
                  pl.BlockSpec((nb, tt, w), lambda bi, i: (bi, i, 2)),
                  pl.BlockSpec((nb, SUBLANES_V7X, w), prev_map(1)),
                  pl.BlockSpec((nb, SUBLANES_V7X, w), prev_map(2)),
                  pl.BlockSpec((nb, 2, w), lambda bi, i: (bi, 0, 0)),
                  pl.BlockSpec((3, w), lambda bi, i: (0, 0))],
        out_specs=(pl.BlockSpec((nb, tt, w), lambda bi, i: (bi, i, 0)),
                   pl.BlockSpec((nb, 2, w), lambda bi, i: (bi, 0, 0))),
        compiler_params=_params(("parallel", "arbitrary"),
                                _vmem_limit(blocks, temps=[((nb, tt, w), F32)] * 6)),
        name="conv",
    )(z_conv, z_conv, z_conv, z_conv, z_conv, conv_prev, w_conv)


def _pattn_kernel(ql_ref, qr_ref, k_ref, kr_ref, o_ref, m_sc, l_sc, acc_sc, *, bq, bk, scale):
    qi = pl.program_id(1)
    kj = pl.program_id(2)
    n_heads, _, c = ql_ref.shape
    rows = n_heads * bq
    last = ((qi + 1) * bq - 1) // bk

    @pl.when(kj == 0)
    def _():
        m_sc[...] = jnp.full(m_sc.shape, -jnp.inf, F32)
        l_sc[...] = jnp.zeros(l_sc.shape, F32)
        acc_sc[...] = jnp.zeros(acc_sc.shape, F32)

    def step(masked):
        q = ql_ref[...].reshape(rows, c)
        qr = qr_ref[...].reshape(rows, qr_ref.shape[-1])
        k = k_ref[...]
        s = (_dot_nt(q, k) + _dot_nt(qr, kr_ref[...])) * scale
        if masked:
            row = lax.broadcasted_iota(jnp.int32, (rows, bk), 0)
            qpos = qi * bq + row % bq
            kpos = kj * bk + lax.broadcasted_iota(jnp.int32, (rows, bk), 1)
            s = jnp.where(kpos <= qpos, s, -jnp.inf)
        m_prev = m_sc[...]
        m_new = jnp.maximum(m_prev, jnp.max(s, axis=-1, keepdims=True))
        corr = jnp.exp(m_prev - m_new)
        p = jnp.exp(s - m_new[:, :1])
        l_sc[...] = corr * l_sc[...] + jnp.sum(p, axis=-1, keepdims=True)
        acc_sc[...] = acc_sc[...] * corr[:, :1] + _dot(p.astype(BF16), k)
        m_sc[...] = m_new

    @pl.when(kj < last)
    def _():
        step(False)

    @pl.when(kj == last)
    def _():
        step(True)
        o = acc_sc[...] / l_sc[:, :1]
        o_ref[...] = o.reshape(n_heads, bq, c).astype(o_ref.dtype)


def _prompt_attention(ql, qr, k16, kr16, batch, seq, scale, bq_pref, bk_pref):
    n_heads, t, c = ql.shape
    dr = qr.shape[-1]
    bq = _pick(seq, bq_pref, 16)
    bk = _pick(seq, bk_pref, bq)
    assert bk % bq == 0
    nq, nk = seq // bq, seq // bk
    rows = n_heads * bq

    def kv_map(b, qi, kj):
        return (b * nk + jnp.minimum(kj, ((qi + 1) * bq - 1) // bk), 0)

    blocks = [((n_heads, bq, c), BF16), ((n_heads, bq, LANES_V7X), BF16), ((bk, c), BF16),
              ((bk, LANES_V7X), BF16), ((n_heads, bq, c), BF16)]
    scratch = [((rows, LANES_V7X), F32), ((rows, LANES_V7X), F32), ((rows, c), F32)]
    return pl.pallas_call(
        functools.partial(_pattn_kernel, bq=bq, bk=bk, scale=scale),
        out_shape=jax.ShapeDtypeStruct((n_heads, t, c), BF16),
        grid=(batch, nq, nk),
        in_specs=[pl.BlockSpec((n_heads, bq, c), lambda b, qi, kj: (0, b * nq + qi, 0)),
                  pl.BlockSpec((n_heads, bq, dr), lambda b, qi, kj: (0, b * nq + qi, 0)),
                  pl.BlockSpec((bk, c), kv_map),
                  pl.BlockSpec((bk, dr), kv_map)],
        out_specs=pl.BlockSpec((n_heads, bq, c), lambda b, qi, kj: (0, b * nq + qi, 0)),
        scratch_shapes=[pltpu.VMEM(s, d) for s, d in scratch],
        compiler_params=_params(("parallel", "parallel", "arbitrary"), _vmem_limit(
            blocks, scratch, temps=[((rows, bk), F32)] * 3 + [((rows, c), F32)] * 2)),
        name="prompt_attn",
    )(ql, qr, k16, kr16)


def _dot_tn(a, b):
    return lax.dot_general(a, b, (((0,), (0,)), ((), ())), preferred_element_type=F32)


def _col_scale(x, v):
    r = x.shape[0]
    vt = jnp.broadcast_to(v, (r, r)).T
    return jnp.concatenate([x[:, k * r:(k + 1) * r] * vt for k in range(x.shape[1] // r)], axis=1)


def _sattn_kernel(pt_ref, ql_ref, qr_ref, cn_ref, kn_ref, *rest, pages, scale):
    del pt_ref
    ck_refs = rest[:pages]
    kt_refs = rest[pages:2 * pages]
    o_ref, qt_sc, qrt_sc, nk_sc, nkr_sc, m_sc, l_sc, acc_sc = rest[2 * pages:]
    j = pl.program_id(1)
    n_heads, tn, c = ql_ref.shape
    rows = n_heads * tn
    dr = qr_ref.shape[-1]
    page = nk_sc.shape[0]

    @pl.when(j == 0)
    def _():
        qt_sc[...] = ql_ref[...].reshape(rows, c).T
        qr = qr_ref[...].reshape(rows, dr)
        qrt_sc[...] = jnp.concatenate([qr, jnp.zeros((rows, rows - dr), F32)], axis=1).T[:dr, :]
        nk_sc[...] = jnp.zeros(nk_sc.shape, F32)
        nk_sc[0:tn, :] = cn_ref[...]
        nkr_sc[...] = jnp.zeros(nkr_sc.shape, F32)
        nkr_sc[0:tn, :] = kn_ref[...]
        nk = nk_sc[...]
        s = (_dot(nk, qt_sc[...]) + _dot(nkr_sc[...], qrt_sc[...])) * scale
        u = lax.broadcasted_iota(jnp.int32, (page, rows), 0)
        t = lax.broadcasted_iota(jnp.int32, (page, rows), 1) % tn
        s = jnp.where(u <= t, s, -jnp.inf)
        m = jnp.max(s, axis=0, keepdims=True)
        p = jnp.exp(s - m)
        m_sc[...] = m
        l_sc[...] = jnp.sum(p, axis=0, keepdims=True)
        acc_sc[...] = _dot_tn(p, nk)

    qt = qt_sc[...]
    qrt = qrt_sc[...]
    s_pages = [(_dot(ck[...], qt) + _dot_tn(kt[...], qrt)) * scale
               for ck, kt in zip(ck_refs, kt_refs)]
    m_prev = m_sc[...]
    m_cur = jnp.max(s_pages[0], axis=0, keepdims=True)
    for s in s_pages[1:]:
        m_cur = jnp.maximum(m_cur, jnp.max(s, axis=0, keepdims=True))
    m_new = jnp.maximum(m_prev, m_cur)
    corr = jnp.exp(m_prev - m_new)
    l_new = corr * l_sc[...]
    pv = None
    for s, ck in zip(s_pages, ck_refs):
        p = jnp.exp(s - m_new)
        l_new = l_new + jnp.sum(p, axis=0, keepdims=True)
        d = _dot_tn(p, ck[...])
        pv = d if pv is None else pv + d
    acc = _col_scale(acc_sc[...], corr) + pv
    m_sc[...] = m_new
    l_sc[...] = l_new
    acc_sc[...] = acc

    @pl.when(j == pl.num_programs(1) - 1)
    def _():
        o_ref[...] = _col_scale(acc, 1.0 / l_new).reshape(n_heads, tn, c)


def _sample_attention(ql, qr, ckv_new, kr_new, cache_ckv, cache_krt, layer, page_table, scale,
                      pages_pref):
    n_heads, t, c = ql.shape
    dr = qr.shape[-1]
    db, n_pages = page_table.shape
    tn = t // db
    page = cache_ckv.shape[2]
    pages = _pick(n_pages, pages_pref, 1)
    rows = n_heads * tn
    assert rows == LANES_V7X and tn <= page and dr <= LANES_V7X

    def page_map(r):
        return lambda b, j, pt: (layer, pt[b * n_pages + j * pages + r], 0, 0)

    in_specs = [pl.BlockSpec((n_heads, tn, c), lambda b, j, pt: (0, b, 0)),
                pl.BlockSpec((n_heads, tn, dr), lambda b, j, pt: (0, b, 0)),
                pl.BlockSpec((tn, c), lambda b, j, pt: (b, 0)),
                pl.BlockSpec((tn, dr), lambda b, j, pt: (b, 0))]
    in_specs += [pl.BlockSpec((None, None, page, c), page_map(r)) for r in range(pages)]
    in_specs += [pl.BlockSpec((None, None, dr, page), page_map(r)) for r in range(pages)]
    blocks = [((n_heads, tn, c), F32), ((n_heads, tn, LANES_V7X), F32), ((tn, c), F32),
              ((tn, LANES_V7X), F32), ((n_heads, tn, c), F32)]
    blocks += [((page, c), F32), ((dr, page), F32)] * pages
    scratch = [((c, rows), F32), ((dr, rows), F32), ((page, c), F32), ((page, dr), F32),
               ((1, rows), F32), ((1, rows), F32), ((rows, c), F32)]
    return pl.pallas_call(
        functools.partial(_sattn_kernel, pages=pages, scale=scale),
        out_shape=jax.ShapeDtypeStruct((n_heads, t, c), F32),
        grid_spec=pltpu.PrefetchScalarGridSpec(
            num_scalar_prefetch=1,
            grid=(db, n_pages // pages),
            in_specs=in_specs,
            out_specs=pl.BlockSpec((n_heads, tn, c), lambda b, j, pt: (0, b, 0)),
            scratch_shapes=[pltpu.VMEM(s, d) for s, d in scratch]),
        compiler_params=_params(("parallel", "arbitrary"), _vmem_limit(
            blocks, scratch, temps=[((page, rows), F32)] * (2 * pages) + [((rows, c), F32)] * 3)),
        name="sample_attn",
    )(page_table.reshape(-1), ql, qr, ckv_new, kr_new, *([cache_ckv] * pages), *([cache_krt] * pages))


def _uv_kernel(ol_ref, w_ref, o_ref):
    dv = w_ref.shape[-1]
    for h in range(ol_ref.shape[0]):
        o_ref[:, h * dv:(h + 1) * dv] = _dot(ol_ref[h].astype(BF16), w_ref[h]).astype(o_ref.dtype)


def _uv(o_lat, w_uvt, tm_pref):
    n_heads, t, c = o_lat.shape
    dv = w_uvt.shape[-1]
    tm = _pick(t, tm_pref, 16)
    blocks = [((n_heads, tm, c), o_lat.dtype), (w_uvt.shape, BF16), ((tm, n_heads * dv), BF16)]
    return pl.pallas_call(
        _uv_kernel,
        out_shape=jax.ShapeDtypeStruct((t, n_heads * dv), BF16),
        grid=(t // tm,),
        in_specs=[pl.BlockSpec((n_heads, tm, c), lambda i: (0, i, 0)),
                  pl.BlockSpec(w_uvt.shape, lambda i: (0, 0, 0))],
        out_specs=pl.BlockSpec((tm, n_heads * dv), lambda i: (i, 0)),
        compiler_params=_params(("parallel",), _vmem_limit(blocks, temps=[((tm, c), BF16)] * 2)),
        name="uv",
    )(o_lat, w_uvt)


def _gate_up_kernel(h_ref, wg_ref, wu_ref, o_ref):
    h = h_ref[...]
    g = _dot(h, wg_ref[...])
    u = _dot(h, wu_ref[...])
    o_ref[...] = (g * jax.nn.sigmoid(g) * u).astype(o_ref.dtype)


def _gate_up(h, w_gate, w_up, tm_pref, tn_pref):
    t, d = h.shape
    f = w_gate.shape[1]
    tm = _pick(t, tm_pref, 16)
    tn = _pick(f, tn_pref, LANES_V7X)
    blocks = [((tm, d), BF16), ((d, tn), BF16), ((d, tn), BF16), ((tm, tn), BF16)]
    return pl.pallas_call(
        _gate_up_kernel,
        out_shape=jax.ShapeDtypeStruct((t, f), BF16),
        grid=(t // tm, f // tn),
        in_specs=[pl.BlockSpec((tm, d), lambda i, j: (i, 0)),
                  pl.BlockSpec((d, tn), lambda i, j: (0, j)),
                  pl.BlockSpec((d, tn), lambda i, j: (0, j))],
        out_specs=pl.BlockSpec((tm, tn), lambda i, j: (i, j)),
        compiler_params=_params(("parallel", "arbitrary"),
                                _vmem_limit(blocks, temps=[((tm, tn), F32)] * 4)),
        name="gate_up",
    )(h, w_gate, w_up)


def _rope_tables(pos, dr):
    inv_freq = ROPE_BASE ** (-jnp.arange(0, dr, 2, dtype=F32) / dr)
    ang = pos[:, None] * inv_freq[None, :]
    cos, sin = jnp.cos(ang), jnp.sin(ang)
    return jnp.concatenate([cos, cos], axis=-1), jnp.concatenate([-sin, sin], axis=-1)


def _prep_weights(w_in, w_uq, w_uk, w_uv, w_o, w_gate, w_up, w_down, rq, c, dr):
    d = w_in.shape[0]
    n_heads, dn = w_uk.shape[1], w_uk.shape[2]
    n_mla = rq + c + dr
    pad = (-n_mla) % LANES_V7X
    w_mla = jnp.concatenate([w_in[:, :n_mla], jnp.zeros((d, pad), w_in.dtype)], axis=1).astype(BF16)
    w_cv = w_in[:, n_mla:].astype(BF16)
    w_uq3 = w_uq.reshape(rq, n_heads, dn + dr)
    w_nope = w_uq3[:, :, :dn].reshape(rq, n_heads * dn).astype(BF16)
    w_rope = w_uq3[:, :, dn:].reshape(rq, n_heads * dr).astype(BF16)
    w_ukt = jnp.transpose(w_uk, (1, 2, 0)).astype(BF16)
    w_uvt = jnp.transpose(w_uv, (1, 0, 2)).astype(BF16)
    return (w_mla, w_cv, w_nope, w_rope, w_ukt, w_uvt, w_o.astype(BF16), w_gate.astype(BF16),
            w_up.astype(BF16), w_down.astype(BF16))


def _block(x, pos, conv_prev, attend, is_prompt, g_attn, g_q, g_kv, w_conv, g_ffn, wts, scale):
    (w_mla, w_cv, w_nope, w_rope, w_ukt, w_uvt, w_o, w_gate, w_up, w_down) = wts
    b, t, d = x.shape
    bt = b * t
    rq, c = g_q.shape[0], g_kv.shape[0]
    dr = w_rope.shape[1] // w_ukt.shape[0]
    cw = w_conv.shape[1]
    x2 = x.reshape(bt, d)
    tm_big = 1024

    h = _rmsnorm(x2, g_attn, BF16)
    z_mla = _mm([(h, 0, d)], [(w_mla, 0, d)], None, F32, 512, w_mla.shape[1], "in_proj_mla")
    z_cv = _mm([(h, 0, d)], [(w_cv, 0, d)], None, F32, tm_big, 512, "in_proj_conv")

    cos2, sin2 = _rope_tables(pos, dr)
    cos2 = jnp.tile(cos2, (b, 1))
    sin2 = jnp.tile(sin2, (b, 1))
    ql, qr = _qpath(z_mla, g_q, w_nope, w_rope, w_ukt, cos2, sin2, BF16 if is_prompt else F32, 256)
    ckv32, ckv16, kr32, kr16 = _kvpath(z_mla, g_kv, cos2, sin2, rq, 512)

    if is_prompt:
        o_lat = _prompt_attention(ql, qr, ckv16, kr16, b, t, scale, 128, 512)
        o_conv, new_conv = _conv(z_cv.reshape(b, t, 3 * cw), conv_prev, w_conv, 1, _pick(t, 512, 8))
    else:
        o_lat = attend(ql, qr, ckv32, kr32)
        o_conv, new_conv = _conv(z_cv.reshape(b, t, 3 * cw), conv_prev, w_conv, _pick(b, 16, 1), t)
    o_attn = _uv(o_lat, w_uvt, 256)
    aw = o_attn.shape[1]
    x1 = _mm([(o_attn, 0, aw), (o_conv.reshape(bt, cw), 0, cw)],
             [(w_o, 0, aw), (w_o, aw // cw, cw)], x2, F32, tm_big, 512, "out_proj")
    h2 = _rmsnorm(x1, g_ffn, BF16)
    act = _gate_up(h2, w_gate, w_up, tm_big, 256)
    f = act.shape[1]
    x_out = _mm([(act, 0, f)], [(w_down, 0, f)], x1, F32, 512, 256, "down_proj")
    return x_out, ckv32, kr32, new_conv


def kernel(x_prompt, x_sample, cache_ckv, cache_krope, state_conv, page_table, g_attn, w_in, g_q,
           w_uq, g_kv, w_uk, w_uv, w_conv, w_o, g_ffn, w_gate, w_up, w_down, g_final):
    depth = w_in.shape[0]
    b, s, d = x_prompt.shape
    db, ts, _ = x_sample.shape
    rq, c = g_q.shape[1], g_kv.shape[1]
    dn = w_uk.shape[3]
    dr = w_uq.shape[2] // w_uk.shape[2] - dn
    cw = w_conv.shape[2]
    aw = w_uv.shape[2] * w_uv.shape[3]
    assert w_o.shape[1] == aw + cw and aw % cw == 0 and rq % c == 0
    past_len = page_table.shape[1] * cache_ckv.shape[2]
    scale = float(dn + dr) ** -0.5
    pos_p = jnp.arange(s, dtype=F32)
    pos_s = past_len + jnp.arange(ts, dtype=F32)

    cache_krt = jnp.swapaxes(cache_krope, -1, -2)

    xp, xs = x_prompt, x_sample
    outs = [[] for _ in range(6)]
    for l in range(depth):
        wts = _prep_weights(w_in[l], w_uq[l], w_uk[l], w_uv[l], w_o[l], w_gate[l], w_up[l],
                            w_down[l], rq, c, dr)
        small = (g_attn[l], g_q[l], g_kv[l], w_conv[l], g_ffn[l])
        conv0 = jnp.zeros((b, w_conv.shape[1] - 1, cw), F32)
        xp, ckv_p, kr_p, cv_p = _block(xp, pos_p, conv0, None, True, *small, wts, scale)
        xp = xp.reshape(b, s, d)
        attend = functools.partial(_sample_attention, cache_ckv=cache_ckv, cache_krt=cache_krt, layer=l,
                                   page_table=page_table, scale=scale, pages_pref=16)
        xs, ckv_s, kr_s, cv_s = _block(xs, pos_s, state_conv[l], attend, False, *small, wts, scale)
        xs = xs.reshape(db, ts, d)
        for acc, val in zip(outs, (ckv_p.reshape(b, s, c), kr_p.reshape(b, s, dr), cv_p,
                                   ckv_s.reshape(db, ts, c), kr_s.reshape(db, ts, dr), cv_s)):
            acc.append(val)
    y_p = _rmsnorm(xp.reshape(b * s, d), g_final, F32).reshape(b, s, d)
    y_s = _rmsnorm(xs.reshape(db * ts, d), g_final, F32).reshape(db, ts, d)
    return (y_p, y_s) + tuple(jnp.stack(o) for o in outs)
```

```python
import functools

import jax
import jax.numpy as jnp
from jax import lax
from jax.experimental import pallas as pl
from jax.experimental.pallas import tpu as pltpu

EPS = 1e-6
ROPE_BASE = 10000.0

F32 = jnp.float32
BF16 = jnp.bfloat16

LANES_V7X = 128
SUBLANES_V7X = 8
VMEM_BYTES_V7X = 64 * 1024 * 1024
VMEM_REQUEST_CAP = VMEM_BYTES_V7X - 6 * 1024 * 1024


def _nbytes(shape, dtype):
    n = 1
    for s in shape:
        n *= s
    return n * jnp.dtype(dtype).itemsize


def _vmem_limit(blocks, scratch=(), temps=()):
    need = 2 * sum(_nbytes(s, d) for s, d in blocks)
    need += sum(_nbytes(s, d) for s, d in scratch)
    need += sum(_nbytes(s, d) for s, d in temps)
    need += 2 * 1024 * 1024
    return int(min(max(need, 16 * 1024 * 1024), VMEM_REQUEST_CAP))


def _pick(total, pref, multiple):
    best = None
    d = multiple
    while d <= min(total, pref):
        if total % d == 0:
            best = d
        d += multiple
    return best if best is not None else total


def _params(semantics, vmem):
    return pltpu.CompilerParams(dimension_semantics=semantics, vmem_limit_bytes=vmem)


def _dot(a, b):
    return jnp.dot(a, b, preferred_element_type=F32)


def _dot_nt(a, b):
    return lax.dot_general(a, b, (((1,), (1,)), ((), ())), preferred_element_type=F32)


def _rms(x, g):
    return x * lax.rsqrt(jnp.mean(x * x, axis=-1, keepdims=True) + EPS) * g


def _rope(x, cos2, sin2):
    half = x.shape[-1] // 2
    swapped = jnp.concatenate([x[:, half:], x[:, :half]], axis=-1)
    return x * cos2 + swapped * sin2


def _rmsnorm_kernel(x_ref, g_ref, o_ref):
    o_ref[...] = _rms(x_ref[...].astype(F32), g_ref[...]).astype(o_ref.dtype)


def _rmsnorm(x, g, out_dtype):
    t, d = x.shape
    tr = _pick(t, 256, SUBLANES_V7X)
    return pl.pallas_call(
        _rmsnorm_kernel,
        out_shape=jax.ShapeDtypeStruct((t, d), out_dtype),
        grid=(t // tr,),
        in_specs=[pl.BlockSpec((tr, d), lambda i: (i, 0)),
                  pl.BlockSpec((1, d), lambda i: (0, 0))],
        out_specs=pl.BlockSpec((tr, d), lambda i: (i, 0)),
        compiler_params=_params(("parallel",), _vmem_limit(
            [((tr, d), x.dtype), ((tr, d), out_dtype)], temps=[((tr, d), F32)] * 2)),
        name="rmsnorm",
    )(x, g.reshape(1, d).astype(F32))


def _mm_kernel(*refs, n_pairs, has_res):
    a_refs = refs[:n_pairs]
    w_refs = refs[n_pairs:2 * n_pairs]
    o_ref = refs[-1]
    acc = _dot(a_refs[0][...], w_refs[0][...])
    for a_ref, w_ref in zip(a_refs[1:], w_refs[1:]):
        acc = acc + _dot(a_ref[...], w_ref[...])
    if has_res:
        acc = acc + refs[2 * n_pairs][...]
    o_ref[...] = acc.astype(o_ref.dtype)


def _mm(a_list, w_list, res, out_dtype, tm_pref, tn_pref, name):
    t = a_list[0][0].shape[0]
    n = w_list[0][0].shape[1]
    tm = _pick(t, tm_pref, 16)
    tn = _pick(n, tn_pref, LANES_V7X)
    in_specs, args, blocks = [], [], []
    for a, cb, kd in a_list:
        in_specs.append(pl.BlockSpec((tm, kd), lambda i, j, cb=cb: (i, cb)))
        args.append(a)
        blocks.append(((tm, kd), a.dtype))
    for w, rb, kd in w_list:
        in_specs.append(pl.BlockSpec((kd, tn), lambda i, j, rb=rb: (rb, j)))
        args.append(w)
        blocks.append(((kd, tn), w.dtype))
    if res is not None:
        in_specs.append(pl.BlockSpec((tm, tn), lambda i, j: (i, j)))
        args.append(res)
        blocks.append(((tm, tn), res.dtype))
    blocks.append(((tm, tn), out_dtype))
    return pl.pallas_call(
        functools.partial(_mm_kernel, n_pairs=len(a_list), has_res=res is not None),
        out_shape=jax.ShapeDtypeStruct((t, n), out_dtype),
        grid=(t // tm, n // tn),
        in_specs=in_specs,
        out_specs=pl.BlockSpec((tm, tn), lambda i, j: (i, j)),
        compiler_params=_params(("parallel", "arbitrary"),
                                _vmem_limit(blocks, temps=[((tm, tn), F32)] * 2)),
        name=name,
    )(*args)


def _qpath_kernel(cq_ref, g_ref, wn_ref, wr_ref, wuk_ref, cos_ref, sin_ref, ql_ref, qr_ref,
                  *, dn, dr):
    y = _rms(cq_ref[...], g_ref[...]).astype(BF16)
    qn = _dot(y, wn_ref[...])
    qr = _dot(y, wr_ref[...])
    cos2 = cos_ref[...]
    sin2 = sin_ref[...]
    for h in range(ql_ref.shape[0]):
        qn_h = qn[:, h * dn:(h + 1) * dn].astype(BF16)
        ql_ref[h] = _dot(qn_h, wuk_ref[h]).astype(ql_ref.dtype)
        qr_ref[h] = _rope(qr[:, h * dr:(h + 1) * dr], cos2, sin2).astype(qr_ref.dtype)


def _qpath(z_mla, g_q, w_nope, w_rope, w_ukt, cos2, sin2, out_dtype, tm_pref):
    t = z_mla.shape[0]
    rq = g_q.shape[0]
    n_heads, dn, c = w_ukt.shape
    dr = w_rope.shape[1] // n_heads
    tm = _pick(t, tm_pref, 16)
    blocks = [((tm, rq), F32), (w_nope.shape, BF16), (w_rope.shape, BF16), (w_ukt.shape, BF16),
              ((tm, LANES_V7X), F32), ((tm, LANES_V7X), F32),
              ((n_heads, tm, c), out_dtype), ((n_heads, tm, LANES_V7X), out_dtype)]
    return pl.pallas_call(
        functools.partial(_qpath_kernel, dn=dn, dr=dr),
        out_shape=(jax.ShapeDtypeStruct((n_heads, t, c), out_dtype),
                   jax.ShapeDtypeStruct((n_heads, t, dr), out_dtype)),
        grid=(t // tm,),
        in_specs=[pl.BlockSpec((tm, rq), lambda i: (i, 0)),
                  pl.BlockSpec((1, rq), lambda i: (0, 0)),
                  pl.BlockSpec(w_nope.shape, lambda i: (0, 0)),
                  pl.BlockSpec(w_rope.shape, lambda i: (0, 0)),
                  pl.BlockSpec(w_ukt.shape, lambda i: (0, 0, 0)),
                  pl.BlockSpec((tm, dr), lambda i: (i, 0)),
                  pl.BlockSpec((tm, dr), lambda i: (i, 0))],
        out_specs=(pl.BlockSpec((n_heads, tm, c), lambda i: (0, i, 0)),
                   pl.BlockSpec((n_heads, tm, dr), lambda i: (0, i, 0))),
        compiler_params=_params(("parallel",), _vmem_limit(
            blocks, temps=[((tm, w_nope.shape[1]), F32), ((tm, w_rope.shape[1]), F32),
                           ((tm, c), F32)])),
        name="qpath",
    )(z_mla, g_q.reshape(1, rq).astype(F32), w_nope, w_rope, w_ukt, cos2, sin2)


def _kvpath_kernel(ckv_ref, kr_ref, g_ref, cos_ref, sin_ref, c32_ref, c16_ref, k32_ref, k16_ref):
    ckv = _rms(ckv_ref[...], g_ref[...])
    c32_ref[...] = ckv
    c16_ref[...] = ckv.astype(BF16)
    dr = k32_ref.shape[-1]
    kr = _rope(kr_ref[:, :dr], cos_ref[...], sin_ref[...])
    k32_ref[...] = kr
    k16_ref[...] = kr.astype(BF16)


def _kvpath(z_mla, g_kv, cos2, sin2, rq, tm_pref):
    t = z_mla.shape[0]
    c = g_kv.shape[0]
    dr = cos2.shape[1]
    tm = _pick(t, tm_pref, 16)
    blocks = [((tm, c), F32), ((tm, LANES_V7X), F32), ((tm, LANES_V7X), F32) , ((tm, LANES_V7X), F32),
              ((tm, c), F32), ((tm, c), BF16), ((tm, LANES_V7X), F32), ((tm, LANES_V7X), BF16)]
    return pl.pallas_call(
        _kvpath_kernel,
        out_shape=(jax.ShapeDtypeStruct((t, c), F32), jax.ShapeDtypeStruct((t, c), BF16),
                   jax.ShapeDtypeStruct((t, dr), F32), jax.ShapeDtypeStruct((t, dr), BF16)),
        grid=(t // tm,),
        in_specs=[pl.BlockSpec((tm, c), lambda i: (i, rq // c)),
                  pl.BlockSpec((tm, LANES_V7X), lambda i: (i, (rq + c) // LANES_V7X)),
                  pl.BlockSpec((1, c), lambda i: (0, 0)),
                  pl.BlockSpec((tm, dr), lambda i: (i, 0)),
                  pl.BlockSpec((tm, dr), lambda i: (i, 0))],
        out_specs=(pl.BlockSpec((tm, c), lambda i: (i, 0)), pl.BlockSpec((tm, c), lambda i: (i, 0)),
                   pl.BlockSpec((tm, dr), lambda i: (i, 0)), pl.BlockSpec((tm, dr), lambda i: (i, 0))),
        compiler_params=_params(("parallel",), _vmem_limit(blocks, temps=[((tm, c), F32)] * 2)),
        name="kvpath",
    )(z_mla, z_mla, g_kv.reshape(1, c).astype(F32), cos2, sin2)


def _conv_kernel(gb_ref, gc_ref, u_ref, gcp_ref, up_ref, cp_ref, w_ref, o_ref, nc_ref):
    i = pl.program_id(1)
    nb, tt, w = gc_ref.shape
    v = gc_ref[...] * u_ref[...]
    vp = gcp_ref[...] * up_ref[...]
    first = i == 0
    hist = cp_ref[...]
    h1 = jnp.where(first, hist[:, 1:2, :], vp[:, 7:8, :])
    h0 = jnp.where(first, hist[:, 0:1, :], vp[:, 6:7, :])
    v2 = v.reshape(nb * tt, w)
    r1 = pltpu.roll(v2, 1, axis=0).reshape(nb, tt, w)
    r2 = pltpu.roll(v2, 2, axis=0).reshape(nb, tt, w)
    tpos = lax.broadcasted_iota(jnp.int32, (nb, tt, w), 1)
    vm1 = jnp.where(tpos == 0, h1, r1)
    vm2 = jnp.where(tpos == 0, h0, jnp.where(tpos == 1, h1, r2))
    wc = w_ref[...]
    y = wc[0:1, :] * vm2 + wc[1:2, :] * vm1 + wc[2:3, :] * v
    o_ref[...] = (gb_ref[...] * y).astype(o_ref.dtype)

    @pl.when(i == pl.num_programs(1) - 1)
    def _():
        nc_ref[...] = v[:, tt - 2:tt, :]


def _conv(z_conv, conv_prev, w_conv, nb, tt):
    b, t, w3 = z_conv.shape
    w = w3 // 3
    assert w_conv.shape[0] == 3 and conv_prev.shape[1] == 2
    pb = tt // SUBLANES_V7X

    def prev_map(col):
        return lambda bi, i: (bi, jnp.maximum(i * pb - 1, 0), col)

    blocks = [((nb, tt, w), F32)] * 3 + [((nb, 8, w), F32)] * 4 + [((nb, tt, w), BF16)]
    return pl.pallas_call(
        _conv_kernel,
        out_shape=(jax.ShapeDtypeStruct((b, t, w), BF16), jax.ShapeDtypeStruct((b, 2, w), F32)),
        grid=(b // nb, t // tt),
        in_specs=[pl.BlockSpec((nb, tt, w), lambda bi, i: (bi, i, 0)),
                  pl.BlockSpec((nb, tt, w), lambda bi, i: (bi, i, 1)),
                  pl.BlockSpec((nb, tt, w), lambda bi, i: (bi, i, 2)),
                  pl.BlockSpec((nb, SUBLANES_V7X, w), prev_map(1)),
                  pl.BlockSpec((nb, SUBLANES_V7X, w), prev_map(2)),
                  pl.BlockSpec((nb, 2, w), lambda bi, i: (bi, 0, 0)),
                  pl.BlockSpec((3, w), lambda bi, i: (0, 0))],
        out_specs=(pl.BlockSpec((nb, tt, w), lambda bi, i: (bi, i, 0)),
                   pl.BlockSpec((nb, 2, w), lambda bi, i: (bi, 0, 0))),
        compiler_params=_params(("parallel", "arbitrary"),
                                _vmem_limit(blocks, temps=[((nb, tt, w), F32)] * 6)),
        name="conv",
    )(z_conv, z_conv, z_conv, z_conv, z_conv, conv_prev, w_conv)


def _pattn_kernel(ql_ref, qr_ref, k_ref, kr_ref, o_ref, m_sc, l_sc, acc_sc, *, bq, bk, scale):
    qi = pl.program_id(1)
    kj = pl.program_id(2)
    n_heads, _, c = ql_ref.shape
    cols = n_heads * bq
    last = ((qi + 1) * bq - 1) // bk

    @pl.when(kj == 0)
    def _():
        m_sc[...] = jnp.full(m_sc.shape, -jnp.inf, F32)
        l_sc[...] = jnp.zeros(l_sc.shape, F32)
        acc_sc[...] = jnp.zeros(acc_sc.shape, F32)

    def step(masked):
        q = ql_ref[...].reshape(cols, c)
        qr = qr_ref[...].reshape(cols, qr_ref.shape[-1])
        k = k_ref[...]
        s = (_dot_nt(k, q) + _dot_nt(kr_ref[...], qr)) * scale
        if masked:
            kpos = kj * bk + lax.broadcasted_iota(jnp.int32, (bk, cols), 0)
            qpos = qi * bq + lax.broadcasted_iota(jnp.int32, (bk, cols), 1) % bq
            s = jnp.where(kpos <= qpos, s, -jnp.inf)
        m_prev = m_sc[...]
        m_new = jnp.maximum(m_prev, jnp.max(s, axis=0, keepdims=True))
        corr = jnp.exp(m_prev - m_new)
        p = jnp.exp(s - m_new)
        l_sc[...] = corr * l_sc[...] + jnp.sum(p, axis=0, keepdims=True)
        acc_sc[...] = acc_sc[...] * corr + _dot_tn(k, p.astype(BF16))
        m_sc[...] = m_new

    @pl.when(kj < last)
    def _():
        step(False)

    @pl.when(kj == last)
    def _():
        step(True)
        o = (acc_sc[...] / l_sc[...]).T
        o_ref[...] = o.reshape(n_heads, bq, c).astype(o_ref.dtype)


def _prompt_attention(ql, qr, k16, kr16, batch, seq, scale, bq_pref, bk_pref):
    n_heads, t, c = ql.shape
    dr = qr.shape[-1]
    bq = _pick(seq, bq_pref, 16)
    bk = _pick(seq, bk_pref, bq)
    assert bk % bq == 0
    nq, nk = seq // bq, seq // bk
    rows = n_heads * bq

    def kv_map(b, qi, kj):
        return (b * nk + jnp.minimum(kj, ((qi + 1) * bq - 1) // bk), 0)

    blocks = [((n_heads, bq, c), BF16), ((n_heads, bq, LANES_V7X), BF16), ((bk, c), BF16),
              ((bk, LANES_V7X), BF16), ((n_heads, bq, c), BF16)]
    scratch = [((1, rows), F32), ((1, rows), F32), ((c, rows), F32)]
    return pl.pallas_call(
        functools.partial(_pattn_kernel, bq=bq, bk=bk, scale=scale),
        out_shape=jax.ShapeDtypeStruct((n_heads, t, c), BF16),
        grid=(batch, nq, nk),
        in_specs=[pl.BlockSpec((n_heads, bq, c), lambda b, qi, kj: (0, b * nq + qi, 0)),
                  pl.BlockSpec((n_heads, bq, dr), lambda b, qi, kj: (0, b * nq + qi, 0)),
                  pl.BlockSpec((bk, c), kv_map),
                  pl.BlockSpec((bk, dr), kv_map)],
        out_specs=pl.BlockSpec((n_heads, bq, c), lambda b, qi, kj: (0, b * nq + qi, 0)),
        scratch_shapes=[pltpu.VMEM(s, d) for s, d in scratch],
        compiler_params=_params(("parallel", "parallel", "arbitrary"), _vmem_limit(
            blocks, scratch, temps=[((rows, bk), F32)] * 3 + [((rows, c), F32)] * 2)),
        name="prompt_attn",
    )(ql, qr, k16, kr16)


def _dot_tn(a, b):
    return lax.dot_general(a, b, (((0,), (0,)), ((), ())), preferred_element_type=F32)


def _col_scale(x, v):
    r = x.shape[0]
    vt = jnp.broadcast_to(v, (r, r)).T
    return jnp.concatenate([x[:, k * r:(k + 1) * r] * vt for k in range(x.shape[1] // r)], axis=1)


def _sattn_kernel(pt_ref, ql_ref, qr_ref, cn_ref, kn_ref, ck_hbm, kt_hbm, o_ref, kbuf, ktbuf, sem,
                  qt_sc, qrt_sc, nk_sc, nkr_sc, m_sc, l_sc, acc_sc, *, pages, layer, scale):
    n_steps = pl.num_programs(1)
    j = pl.program_id(1)
    g = pl.program_id(0) * n_steps + j
    total = pl.num_programs(0) * n_steps
    n_heads, tn, c = ql_ref.shape
    rows = n_heads * tn
    dr = qr_ref.shape[-1]
    page = nk_sc.shape[0]

    def page_copies(step, slot):
        cps = []
        for r in range(pages):
            pg = pt_ref[step * pages + r]
            cps.append(pltpu.make_async_copy(
                ck_hbm.at[layer, pg], kbuf.at[slot, pl.ds(r * page, page), :], sem.at[slot]))
            cps.append(pltpu.make_async_copy(
                kt_hbm.at[layer, pg], ktbuf.at[slot, :, pl.ds(r * page, page)], sem.at[slot]))
        return cps

    @pl.when(g == 0)
    def _():
        for cp in page_copies(0, 0):
            cp.start()

    @pl.when(j == 0)
    def _():
        qt_sc[...] = ql_ref[...].reshape(rows, c).T
        qr = qr_ref[...].reshape(rows, dr)
        qrt_sc[...] = jnp.concatenate([qr, jnp.zeros((rows, rows - dr), F32)], axis=1).T[:dr, :]
        nk_sc[...] = jnp.zeros(nk_sc.shape, F32)
        nk_sc[0:tn, :] = cn_ref[...]
        nkr_sc[...] = jnp.zeros(nkr_sc.shape, F32)
        nkr_sc[0:tn, :] = kn_ref[...]
        nk = nk_sc[...]
        s = (_dot(nk, qt_sc[...]) + _dot(nkr_sc[...], qrt_sc[...])) * scale
        u = lax.broadcasted_iota(jnp.int32, (page, rows), 0)
        t = lax.broadcasted_iota(jnp.int32, (page, rows), 1) % tn
        s = jnp.where(u <= t, s, -jnp.inf)
        m = jnp.max(s, axis=0, keepdims=True)
        p = jnp.exp(s - m)
        m_sc[...] = m
        l_sc[...] = jnp.sum(p, axis=0, keepdims=True)
        acc_sc[...] = _dot_tn(p, nk)

    def attend(slot):
        @pl.when(g + 1 < total)
        def _():
            for cp in page_copies(g + 1, 1 - slot):
                cp.start()

        for cp in page_copies(g, slot):
            cp.wait()
        k = kbuf[slot]
        s = (_dot(k, qt_sc[...]) + _dot_tn(ktbuf[slot], qrt_sc[...])) * scale
        m_prev = m_sc[...]
        m_new = jnp.maximum(m_prev, jnp.max(s, axis=0, keepdims=True))
        corr = jnp.exp(m_prev - m_new)
        p = jnp.exp(s - m_new)
        l_new = corr * l_sc[...] + jnp.sum(p, axis=0, keepdims=True)
        acc = _col_scale(acc_sc[...], corr) + _dot_tn(p, k)
        m_sc[...] = m_new
        l_sc[...] = l_new
        acc_sc[...] = acc

        @pl.when(j == n_steps - 1)
        def _():
            o_ref[...] = _col_scale(acc, 1.0 / l_new).reshape(n_heads, tn, c)

    for slot in range(2):
        pl.when(g % 2 == slot)(functools.partial(attend, slot))


def _sample_attention(ql, qr, ckv_new, kr_new, cache_ckv, cache_krt, layer, page_table, scale,
                      pages_pref):
    n_heads, t, c = ql.shape
    dr = qr.shape[-1]
    db, n_pages = page_table.shape
    tn = t // db
    page = cache_ckv.shape[2]
    pages = _pick(n_pages, pages_pref, 1)
    rows = n_heads * tn
    assert rows == LANES_V7X and tn <= page and dr <= LANES_V7X

    in_specs = [pl.BlockSpec((n_heads, tn, c), lambda b, j, pt: (0, b, 0)),
                pl.BlockSpec((n_heads, tn, dr), lambda b, j, pt: (0, b, 0)),
                pl.BlockSpec((tn, c), lambda b, j, pt: (b, 0)),
                pl.BlockSpec((tn, dr), lambda b, j, pt: (b, 0)),
                pl.BlockSpec(memory_space=pl.ANY),
                pl.BlockSpec(memory_space=pl.ANY)]
    blocks = [((n_heads, tn, c), F32), ((n_heads, tn, LANES_V7X), F32), ((tn, c), F32),
              ((tn, LANES_V7X), F32), ((n_heads, tn, c), F32)]
    scratch = [((2, pages * page, c), F32), ((2, dr, pages * page), F32),
               ((c, rows), F32), ((dr, rows), F32), ((page, c), F32), ((page, dr), F32),
               ((1, rows), F32), ((1, rows), F32), ((rows, c), F32)]
    scratch_shapes = [pltpu.VMEM(s, d) for s, d in scratch]
    scratch_shapes.insert(2, pltpu.SemaphoreType.DMA((2,)))
    return pl.pallas_call(
        functools.partial(_sattn_kernel, pages=pages, layer=layer, scale=scale),
        out_shape=jax.ShapeDtypeStruct((n_heads, t, c), F32),
        grid_spec=pltpu.PrefetchScalarGridSpec(
            num_scalar_prefetch=1,
            grid=(db, n_pages // pages),
            in_specs=in_specs,
            out_specs=pl.BlockSpec((n_heads, tn, c), lambda b, j, pt: (0, b, 0)),
            scratch_shapes=scratch_shapes),
        compiler_params=_params(("arbitrary", "arbitrary"), _vmem_limit(
            blocks, scratch, temps=[((pages * page, rows), F32)] * 3 + [((rows, c), F32)] * 3)),
        name="sample_attn",
    )(page_table.reshape(-1), ql, qr, ckv_new, kr_new, cache_ckv, cache_krt)


def _uv_kernel(ol_ref, w_ref, o_ref):
    dv = w_ref.shape[-1]
    for h in range(ol_ref.shape[0]):
        o_ref[:, h * dv:(h + 1) * dv] = _dot(ol_ref[h].astype(BF16), w_ref[h]).astype(o_ref.dtype)


def _uv(o_lat, w_uvt, tm_pref):
    n_heads, t, c = o_lat.shape
    dv = w_uvt.shape[-1]
    tm = _pick(t, tm_pref, 16)
    blocks = [((n_heads, tm, c), o_lat.dtype), (w_uvt.shape, BF16), ((tm, n_heads * dv), BF16)]
    return pl.pallas_call(
        _uv_kernel,
        out_shape=jax.ShapeDtypeStruct((t, n_heads * dv), BF16),
        grid=(t // tm,),
        in_specs=[pl.BlockSpec((n_heads, tm, c), lambda i: (0, i, 0)),
                  pl.BlockSpec(w_uvt.shape, lambda i: (0, 0, 0))],
        out_specs=pl.BlockSpec((tm, n_heads * dv), lambda i: (i, 0)),
        compiler_params=_params(("parallel",), _vmem_limit(blocks, temps=[((tm, c), BF16)] * 2)),
        name="uv",
    )(o_lat, w_uvt)


def _gate_up_kernel(h_ref, wg_ref, wu_ref, o_ref):
    h = h_ref[...]
    g = _dot(h, wg_ref[...])
    u = _dot(h, wu_ref[...])
    o_ref[...] = (g * jax.nn.sigmoid(g) * u).astype(o_ref.dtype)


def _gate_up(h, w_gate, w_up, tm_pref, tn_pref):
    t, d = h.shape
    f = w_gate.shape[1]
    tm = _pick(t, tm_pref, 16)
    tn = _pick(f, tn_pref, LANES_V7X)
    blocks = [((tm, d), BF16), ((d, tn), BF16), ((d, tn), BF16), ((tm, tn), BF16)]
    return pl.pallas_call(
        _gate_up_kernel,
        out_shape=jax.ShapeDtypeStruct((t, f), BF16),
        grid=(t // tm, f // tn),
        in_specs=[pl.BlockSpec((tm, d), lambda i, j: (i, 0)),
                  pl.BlockSpec((d, tn), lambda i, j: (0, j)),
                  pl.BlockSpec((d, tn), lambda i, j: (0, j))],
        out_specs=pl.BlockSpec((tm, tn), lambda i, j: (i, j)),
        compiler_params=_params(("parallel", "arbitrary"),
                                _vmem_limit(blocks, temps=[((tm, tn), F32)] * 4)),
        name="gate_up",
    )(h, w_gate, w_up)


def _rope_tables(pos, dr):
    inv_freq = ROPE_BASE ** (-jnp.arange(0, dr, 2, dtype=F32) / dr)
    ang = pos[:, None] * inv_freq[None, :]
    cos, sin = jnp.cos(ang), jnp.sin(ang)
    return jnp.concatenate([cos, cos], axis=-1), jnp.concatenate([-sin, sin], axis=-1)


def _prep_weights(w_in, w_uq, w_uk, w_uv, w_o, w_gate, w_up, w_down, rq, c, dr):
    d = w_in.shape[0]
    n_heads, dn = w_uk.shape[1], w_uk.shape[2]
    n_mla = rq + c + dr
    pad = (-n_mla) % LANES_V7X
    w_mla = jnp.concatenate([w_in[:, :n_mla], jnp.zeros((d, pad), w_in.dtype)], axis=1).astype(BF16)
    w_cv = w_in[:, n_mla:].astype(BF16)
    w_uq3 = w_uq.reshape(rq, n_heads, dn + dr)
    w_nope = w_uq3[:, :, :dn].reshape(rq, n_heads * dn).astype(BF16)
    w_rope = w_uq3[:, :, dn:].reshape(rq, n_heads * dr).astype(BF16)
    w_ukt = jnp.transpose(w_uk, (1, 2, 0)).astype(BF16)
    w_uvt = jnp.transpose(w_uv, (1, 0, 2)).astype(BF16)
    return (w_mla, w_cv, w_nope, w_rope, w_ukt, w_uvt, w_o.astype(BF16), w_gate.astype(BF16),
            w_up.astype(BF16), w_down.astype(BF16))


def _block(x, pos, conv_prev, attend, is_prompt, g_attn, g_q, g_kv, w_conv, g_ffn, wts, scale):
    (w_mla, w_cv, w_nope, w_rope, w_ukt, w_uvt, w_o, w_gate, w_up, w_down) = wts
    b, t, d = x.shape
    bt = b * t
    rq, c = g_q.shape[0], g_kv.shape[0]
    dr = w_rope.shape[1] // w_ukt.shape[0]
    cw = w_conv.shape[1]
    x2 = x.reshape(bt, d)
    tm_big = 1024

    h = _rmsnorm(x2, g_attn, BF16)
    z_mla = _mm([(h, 0, d)], [(w_mla, 0, d)], None, F32, 512, w_mla.shape[1], "in_proj_mla")
    z_cv = _mm([(h, 0, d)], [(w_cv, 0, d)], None, F32, tm_big, 512, "in_proj_conv")

    cos2, sin2 = _rope_tables(pos, dr)
    cos2 = jnp.tile(cos2, (b, 1))
    sin2 = jnp.tile(sin2, (b, 1))
    ql, qr = _qpath(z_mla, g_q, w_nope, w_rope, w_ukt, cos2, sin2, BF16 if is_prompt else F32, 256)
    ckv32, ckv16, kr32, kr16 = _kvpath(z_mla, g_kv, cos2, sin2, rq, 512)

    if is_prompt:
        o_lat = _prompt_attention(ql, qr, ckv16, kr16, b, t, scale, 256, 512)
        o_conv, new_conv = _conv(z_cv.reshape(b, t, 3 * cw), conv_prev, w_conv, 1, _pick(t, 512, 8))
    else:
        o_lat = attend(ql, qr, ckv32, kr32)
        o_conv, new_conv = _conv(z_cv.reshape(b, t, 3 * cw), conv_prev, w_conv, _pick(b, 16, 1), t)
    o_attn = _uv(o_lat, w_uvt, 256)
    aw = o_attn.shape[1]
    x1 = _mm([(o_attn, 0, aw), (o_conv.reshape(bt, cw), 0, cw)],
             [(w_o, 0, aw), (w_o, aw // cw, cw)], x2, F32, tm_big, 512, "out_proj")
    h2 = _rmsnorm(x1, g_ffn, BF16)
    act = _gate_up(h2, w_gate, w_up, tm_big, 256)
    f = act.shape[1]
    x_out = _mm([(act, 0, f)], [(w_down, 0, f)], x1, F32, 512, 256, "down_proj")
    return x_out, ckv32, kr32, new_conv


def kernel(x_prompt, x_sample, cache_ckv, cache_krope, state_conv, page_table, g_attn, w_in, g_q,
           w_uq, g_kv, w_uk, w_uv, w_conv, w_o, g_ffn, w_gate, w_up, w_down, g_final):
    depth = w_in.shape[0]
    b, s, d = x_prompt.shape
    db, ts, _ = x_sample.shape
    rq, c = g_q.shape[1], g_kv.shape[1]
    dn = w_uk.shape[3]
    dr = w_uq.shape[2] // w_uk.shape[2] - dn
    cw = w_conv.shape[2]
    aw = w_uv.shape[2] * w_uv.shape[3]
    assert w_o.shape[1] == aw + cw and aw % cw == 0 and rq % c == 0
    past_len = page_table.shape[1] * cache_ckv.shape[2]
    scale = float(dn + dr) ** -0.5
    pos_p = jnp.arange(s, dtype=F32)
    pos_s = past_len + jnp.arange(ts, dtype=F32)

    cache_krt = jnp.swapaxes(cache_krope, -1, -2)

    xp, xs = x_prompt, x_sample
    outs = [[] for _ in range(6)]
    for l in range(depth):
        wts = _prep_weights(w_in[l], w_uq[l], w_uk[l], w_uv[l], w_o[l], w_gate[l], w_up[l],
                            w_down[l], rq, c, dr)
        small = (g_attn[l], g_q[l], g_kv[l], w_conv[l], g_ffn[l])
        conv0 = jnp.zeros((b, w_conv.shape[1] - 1, cw), F32)
        xp, ckv_p, kr_p, cv_p = _block(xp, pos_p, conv0, None, True, *small, wts, scale)
        xp = xp.reshape(b, s, d)
        attend = functools.partial(_sample_attention, cache_ckv=cache_ckv, cache_krt=cache_krt, layer=l,
                                   page_table=page_table, scale=scale, pages_pref=32)
        xs, ckv_s, kr_s, cv_s = _block(xs, pos_s, state_conv[l], attend, False, *small, wts, scale)
        xs = xs.reshape(db, ts, d)
        for acc, val in zip(outs, (ckv_p.reshape(b, s, c), kr_p.reshape(b, s, dr), cv_p,
                                   ckv_s.reshape(db, ts, c), kr_s.reshape(db, ts, dr), cv_s)):
            acc.append(val)
    y_p = _rmsnorm(xp.reshape(b * s, d), g_final, F32).reshape(b, s, d)
    y_s = _rmsnorm(xs.reshape(db * ts, d), g_final, F32).reshape(db, ts, d)
    return (y_p, y_s) + tuple(jnp.stack(o) for o in outs)
```

```python
import functools

import jax
import jax.numpy as jnp
from jax import lax
from jax.experimental import pallas as pl
from jax.experimental.pallas import tpu as pltpu

EPS = 1e-6
ROPE_BASE = 10000.0

F32 = jnp.float32
BF16 = jnp.bfloat16

LANES_V7X = 128
SUBLANES_V7X = 8
VMEM_BYTES_V7X = 64 * 1024 * 1024
VMEM_REQUEST_CAP = VMEM_BYTES_V7X - 6 * 1024 * 1024


def _nbytes(shape, dtype):
    n = 1
    for s in shape:
        n *= s
    return n * jnp.dtype(dtype).itemsize


def _vmem_limit(blocks, scratch=(), temps=()):
    need = 2 * sum(_nbytes(s, d) for s, d in blocks)
    need += sum(_nbytes(s, d) for s, d in scratch)
    need += sum(_nbytes(s, d) for s, d in temps)
    need += 2 * 1024 * 1024
    return int(min(max(need, 16 * 1024 * 1024), VMEM_REQUEST_CAP))


def _pick(total, pref, multiple):
    best = None
    d = multiple
    while d <= min(total, pref):
        if total % d == 0:
            best = d
        d += multiple
    return best if best is not None else total


def _params(semantics, vmem):
    return pltpu.CompilerParams(dimension_semantics=semantics, vmem_limit_bytes=vmem)


def _dot(a, b):
    return jnp.dot(a, b, preferred_element_type=F32)


def _dot_nt(a, b):
    return lax.dot_general(a, b, (((1,), (1,)), ((), ())), preferred_element_type=F32)


def _rms(x, g):
    return x * lax.rsqrt(jnp.mean(x * x, axis=-1, keepdims=True) + EPS) * g


def _rope(x, cos2, sin2):
    half = x.shape[-1] // 2
    swapped = jnp.concatenate([x[:, half:], x[:, :half]], axis=-1)
    return x * cos2 + swapped * sin2


def _rmsnorm_kernel(x_ref, g_ref, o_ref):
    o_ref[...] = _rms(x_ref[...].astype(F32), g_ref[...]).astype(o_ref.dtype)


def _rmsnorm(x, g, out_dtype):
    t, d = x.shape
    tr = _pick(t, 256, SUBLANES_V7X)
    return pl.pallas_call(
        _rmsnorm_kernel,
        out_shape=jax.ShapeDtypeStruct((t, d), out_dtype),
        grid=(t // tr,),
        in_specs=[pl.BlockSpec((tr, d), lambda i: (i, 0)),
                  pl.BlockSpec((1, d), lambda i: (0, 0))],
        out_specs=pl.BlockSpec((tr, d), lambda i: (i, 0)),
        compiler_params=_params(("parallel",), _vmem_limit(
            [((tr, d), x.dtype), ((tr, d), out_dtype)], temps=[((tr, d), F32)] * 2)),
        name="rmsnorm",
    )(x, g.reshape(1, d).astype(F32))


def _mm_kernel(*refs, n_pairs, has_res):
    a_refs = refs[:n_pairs]
    w_refs = refs[n_pairs:2 * n_pairs]
    o_ref = refs[-1]
    acc = _dot(a_refs[0][...], w_refs[0][...])
    for a_ref, w_ref in zip(a_refs[1:], w_refs[1:]):
        acc = acc + _dot(a_ref[...], w_ref[...])
    if has_res:
        acc = acc + refs[2 * n_pairs][...]
    o_ref[...] = acc.astype(o_ref.dtype)


def _mm(a_list, w_list, res, out_dtype, tm_pref, tn_pref, name):
    t = a_list[0][0].shape[0]
    n = w_list[0][0].shape[1]
    tm = _pick(t, tm_pref, 16)
    tn = _pick(n, tn_pref, LANES_V7X)
    in_specs, args, blocks = [], [], []
    for a, cb, kd in a_list:
        in_specs.append(pl.BlockSpec((tm, kd), lambda i, j, cb=cb: (i, cb)))
        args.append(a)
        blocks.append(((tm, kd), a.dtype))
    for w, rb, kd in w_list:
        in_specs.append(pl.BlockSpec((kd, tn), lambda i, j, rb=rb: (rb, j)))
        args.append(w)
        blocks.append(((kd, tn), w.dtype))
    if res is not None:
        in_specs.append(pl.BlockSpec((tm, tn), lambda i, j: (i, j)))
        args.append(res)
        blocks.append(((tm, tn), res.dtype))
    blocks.append(((tm, tn), out_dtype))
    return pl.pallas_call(
        functools.partial(_mm_kernel, n_pairs=len(a_list), has_res=res is not None),
        out_shape=jax.ShapeDtypeStruct((t, n), out_dtype),
        grid=(t // tm, n // tn),
        in_specs=in_specs,
        out_specs=pl.BlockSpec((tm, tn), lambda i, j: (i, j)),
        compiler_params=_params(("parallel", "arbitrary"),
                                _vmem_limit(blocks, temps=[((tm, tn), F32)] * 2)),
        name=name,
    )(*args)


def _norm_proj_kernel(x_ref, g_ref, w_ref, h_ref, z_ref):
    h = _rms(x_ref[...], g_ref[...]).astype(h_ref.dtype)
    h_ref[...] = h
    z_ref[...] = _dot(h, w_ref[...])


def _norm_proj(x, g, w, tm_pref):
    t, d = x.shape
    n = w.shape[1]
    tm = _pick(t, tm_pref, 16)
    blocks = [((tm, d), F32), ((tm, d), BF16), ((tm, n), F32)]
    return pl.pallas_call(
        _norm_proj_kernel,
        out_shape=(jax.ShapeDtypeStruct((t, d), BF16), jax.ShapeDtypeStruct((t, n), F32)),
        grid=(t // tm,),
        in_specs=[pl.BlockSpec((tm, d), lambda i: (i, 0)),
                  pl.BlockSpec((1, d), lambda i: (0, 0)),
                  pl.BlockSpec((d, n), lambda i: (0, 0), pipeline_mode=pl.Buffered(1))],
        out_specs=(pl.BlockSpec((tm, d), lambda i: (i, 0)), pl.BlockSpec((tm, n), lambda i: (i, 0))),
        compiler_params=_params(("parallel",), _vmem_limit(
            blocks, scratch=[((d, n), BF16)], temps=[((tm, d), F32)] * 2 + [((tm, n), F32)])),
        name="norm_in_proj_mla",
    )(x, g.reshape(1, d).astype(F32), w)


def _proj_norm_kernel(a1_ref, a2_ref, w1_ref, w2_ref, res_ref, g_ref, x_ref, h_ref, *, tn):
    j = pl.program_id(1)
    col = pl.multiple_of(j * tn, tn)
    x_ref[:, pl.ds(col, tn)] = (_dot(a1_ref[...], w1_ref[...]) + _dot(a2_ref[...], w2_ref[...])
                                + res_ref[...])

    @pl.when(j == pl.num_programs(1) - 1)
    def _():
        h_ref[...] = _rms(x_ref[...], g_ref[...]).astype(h_ref.dtype)


def _proj_norm(a1, a2, w, res, g, tm_pref, tn_pref):
    t, k1 = a1.shape
    k2 = a2.shape[1]
    n = w.shape[1]
    assert k1 % k2 == 0 and w.shape[0] == k1 + k2
    tm = _pick(t, tm_pref, 16)
    tn = _pick(n, tn_pref, LANES_V7X)
    blocks = [((tm, k1), BF16), ((tm, k2), BF16), ((k1, tn), BF16), ((k2, tn), BF16),
              ((tm, tn), F32), ((tm, n), F32), ((tm, n), BF16)]
    return pl.pallas_call(
        functools.partial(_proj_norm_kernel, tn=tn),
        out_shape=(jax.ShapeDtypeStruct((t, n), F32), jax.ShapeDtypeStruct((t, n), BF16)),
        grid=(t // tm, n // tn),
        in_specs=[pl.BlockSpec((tm, k1), lambda i, j: (i, 0)),
                  pl.BlockSpec((tm, k2), lambda i, j: (i, 0)),
                  pl.BlockSpec((k1, tn), lambda i, j: (0, j)),
                  pl.BlockSpec((k2, tn), lambda i, j: (k1 // k2, j)),
                  pl.BlockSpec((tm, tn), lambda i, j: (i, j)),
                  pl.BlockSpec((1, n), lambda i, j: (0, 0))],
        out_specs=(pl.BlockSpec((tm, n), lambda i, j: (i, 0)), pl.BlockSpec((tm, n), lambda i, j: (i, 0))),
        compiler_params=_params(("parallel", "arbitrary"),
                                _vmem_limit(blocks, temps=[((tm, tn), F32)] * 2 + [((tm, n), F32)])),
        name="out_proj_norm",
    )(a1, a2, w, w, res, g.reshape(1, n).astype(F32))


def _qpath_kernel(cq_ref, g_ref, wn_ref, wr_ref, wuk_ref, cos_ref, sin_ref, ql_ref, qr_ref,
                  *, dn, dr):
    y = _rms(cq_ref[...], g_ref[...]).astype(BF16)
    qn = _dot(y, wn_ref[...])
    qr = _dot(y, wr_ref[...])
    cos2 = cos_ref[...]
    sin2 = sin_ref[...]
    for h in range(ql_ref.shape[0]):
        qn_h = qn[:, h * dn:(h + 1) * dn].astype(BF16)
        ql_ref[h] = _dot(qn_h, wuk_ref[h]).astype(ql_ref.dtype)
        qr_ref[h] = _rope(qr[:, h * dr:(h + 1) * dr], cos2, sin2).astype(qr_ref.dtype)


def _qpath(z_mla, g_q, w_nope, w_rope, w_ukt, cos2, sin2, out_dtype, tm_pref):
    t = z_mla.shape[0]
    rq = g_q.shape[0]
    n_heads, dn, c = w_ukt.shape
    dr = w_rope.shape[1] // n_heads
    tm = _pick(t, tm_pref, 16)
    blocks = [((tm, rq), F32), (w_nope.shape, BF16), (w_rope.shape, BF16), (w_ukt.shape, BF16),
              ((tm, LANES_V7X), F32), ((tm, LANES_V7X), F32),
              ((n_heads, tm, c), out_dtype), ((n_heads, tm, LANES_V7X), out_dtype)]
    return pl.pallas_call(
        functools.partial(_qpath_kernel, dn=dn, dr=dr),
        out_shape=(jax.ShapeDtypeStruct((n_heads, t, c), out_dtype),
                   jax.ShapeDtypeStruct((n_heads, t, dr), out_dtype)),
        grid=(t // tm,),
        in_specs=[pl.BlockSpec((tm, rq), lambda i: (i, 0)),
                  pl.BlockSpec((1, rq), lambda i: (0, 0)),
                  pl.BlockSpec(w_nope.shape, lambda i: (0, 0)),
                  pl.BlockSpec(w_rope.shape, lambda i: (0, 0)),
                  pl.BlockSpec(w_ukt.shape, lambda i: (0, 0, 0)),
                  pl.BlockSpec((tm, dr), lambda i: (i, 0)),
                  pl.BlockSpec((tm, dr), lambda i: (i, 0))],
        out_specs=(pl.BlockSpec((n_heads, tm, c), lambda i: (0, i, 0)),
                   pl.BlockSpec((n_heads, tm, dr), lambda i: (0, i, 0))),
        compiler_params=_params(("parallel",), _vmem_limit(
            blocks, temps=[((tm, w_nope.shape[1]), F32), ((tm, w_rope.shape[1]), F32),
                           ((tm, c), F32)])),
        name="qpath",
    )(z_mla, g_q.reshape(1, rq).astype(F32), w_nope, w_rope, w_ukt, cos2, sin2)


def _kvpath_kernel(ckv_ref, kr_ref, g_ref, cos_ref, sin_ref, c32_ref, c16_ref, k32_ref, k16_ref):
    ckv = _rms(ckv_ref[...], g_ref[...])
    c32_ref[...] = ckv
    c16_ref[...] = ckv.astype(BF16)
    dr = k32_ref.shape[-1]
    kr = _rope(kr_ref[:, :dr], cos_ref[...], sin_ref[...])
    k32_ref[...] = kr
    k16_ref[...] = kr.astype(BF16)


def _kvpath(z_mla, g_kv, cos2, sin2, rq, tm_pref):
    t = z_mla.shape[0]
    c = g_kv.shape[0]
    dr = cos2.shape[1]
    tm = _pick(t, tm_pref, 16)
    blocks = [((tm, c), F32), ((tm, LANES_V7X), F32), ((tm, LANES_V7X), F32) , ((tm, LANES_V7X), F32),
              ((tm, c), F32), ((tm, c), BF16), ((tm, LANES_V7X), F32), ((tm, LANES_V7X), BF16)]
    return pl.pallas_call(
        _kvpath_kernel,
        out_shape=(jax.ShapeDtypeStruct((t, c), F32), jax.ShapeDtypeStruct((t, c), BF16),
                   jax.ShapeDtypeStruct((t, dr), F32), jax.ShapeDtypeStruct((t, dr), BF16)),
        grid=(t // tm,),
        in_specs=[pl.BlockSpec((tm, c), lambda i: (i, rq // c)),
                  pl.BlockSpec((tm, LANES_V7X), lambda i: (i, (rq + c) // LANES_V7X)),
                  pl.BlockSpec((1, c), lambda i: (0, 0)),
                  pl.BlockSpec((tm, dr), lambda i: (i, 0)),
                  pl.BlockSpec((tm, dr), lambda i: (i, 0))],
        out_specs=(pl.BlockSpec((tm, c), lambda i: (i, 0)), pl.BlockSpec((tm, c), lambda i: (i, 0)),
                   pl.BlockSpec((tm, dr), lambda i: (i, 0)), pl.BlockSpec((tm, dr), lambda i: (i, 0))),
        compiler_params=_params(("parallel",), _vmem_limit(blocks, temps=[((tm, c), F32)] * 2)),
        name="kvpath",
    )(z_mla, z_mla, g_kv.reshape(1, c).astype(F32), cos2, sin2)


def _conv_kernel(gb_ref, gc_ref, u_ref, gcp_ref, up_ref, cp_ref, w_ref, o_ref, nc_ref):
    i = pl.program_id(1)
    nb, tt, w = gc_ref.shape
    v = gc_ref[...] * u_ref[...]
    vp = gcp_ref[...] * up_ref[...]
    first = i == 0
    hist = cp_ref[...]
    h1 = jnp.where(first, hist[:, 1:2, :], vp[:, 7:8, :])
    h0 = jnp.where(first, hist[:, 0:1, :], vp[:, 6:7, :])
    v2 = v.reshape(nb * tt, w)
    r1 = pltpu.roll(v2, 1, axis=0).reshape(nb, tt, w)
    r2 = pltpu.roll(v2, 2, axis=0).reshape(nb, tt, w)
    tpos = lax.broadcasted_iota(jnp.int32, (nb, tt, w), 1)
    vm1 = jnp.where(tpos == 0, h1, r1)
    vm2 = jnp.where(tpos == 0, h0, jnp.where(tpos == 1, h1, r2))
    wc = w_ref[...]
    y = wc[0:1, :] * vm2 + wc[1:2, :] * vm1 + wc[2:3, :] * v
    o_ref[...] = (gb_ref[...] * y).astype(o_ref.dtype)

    @pl.when(i == pl.num_programs(1) - 1)
    def _():
        nc_ref[...] = v[:, tt - 2:tt, :]


def _conv(z_conv, conv_prev, w_conv, nb, tt):
    b, t, w3 = z_conv.shape
    w = w3 // 3
    assert w_conv.shape[0] == 3 and conv_prev.shape[1] == 2
    pb = tt // SUBLANES_V7X

    def prev_map(col):
        return lambda bi, i: (bi, jnp.maximum(i * pb - 1, 0), col)

    blocks = [((nb, tt, w), F32)] * 3 + [((nb, 8, w), F32)] * 4 + [((nb, tt, w), BF16)]
    return pl.pallas_call(
        _conv_kernel,
        out_shape=(jax.ShapeDtypeStruct((b, t, w), BF16), jax.ShapeDtypeStruct((b, 2, w), F32)),
        grid=(b // nb, t // tt),
        in_specs=[pl.BlockSpec((nb, tt, w), lambda bi, i: (bi, i, 0)),
                  pl.BlockSpec((nb, tt, w), lambda bi, i: (bi, i, 1)),
                  pl.BlockSpec((nb, tt, w), lambda bi, i: (bi, i, 2)),
                  pl.BlockSpec((nb, SUBLANES_V7X, w), prev_map(1)),
                  pl.BlockSpec((nb, SUBLANES_V7X, w), prev_map(2)),
                  pl.BlockSpec((nb, 2, w), lambda bi, i: (bi, 0, 0)),
                  pl.BlockSpec((3, w), lambda bi, i: (0, 0))],
        out_specs=(pl.BlockSpec((nb, tt, w), lambda bi, i: (bi, i, 0)),
                   pl.BlockSpec((nb, 2, w), lambda bi, i: (bi, 0, 0))),
        compiler_params=_params(("parallel", "arbitrary"),
                                _vmem_limit(blocks, temps=[((nb, tt, w), F32)] * 6)),
        name="conv",
    )(z_conv, z_conv, z_conv, z_conv, z_conv, conv_prev, w_conv)


def _pattn_kernel(ql_ref, qr_ref, k_ref, kr_ref, o_ref, m_sc, l_sc, acc_sc, *, bq, bk, scale):
    qi = pl.program_id(1)
    kj = pl.program_id(2)
    n_heads, _, c = ql_ref.shape
    cols = n_heads * bq
    last = ((qi + 1) * bq - 1) // bk

    @pl.when(kj == 0)
    def _():
        m_sc[...] = jnp.full(m_sc.shape, -jnp.inf, F32)
        l_sc[...] = jnp.zeros(l_sc.shape, F32)
        acc_sc[...] = jnp.zeros(acc_sc.shape, F32)

    def step(masked):
        q = ql_ref[...].reshape(cols, c)
        qr = qr_ref[...].reshape(cols, qr_ref.shape[-1])
        k = k_ref[...]
        s = (_dot_nt(k, q) + _dot_nt(kr_ref[...], qr)) * scale
        if masked:
            kpos = kj * bk + lax.broadcasted_iota(jnp.int32, (bk, cols), 0)
            qpos = qi * bq + lax.broadcasted_iota(jnp.int32, (bk, cols), 1) % bq
            s = jnp.where(kpos <= qpos, s, -jnp.inf)
        m_prev = m_sc[...]
        m_new = jnp.maximum(m_prev, jnp.max(s, axis=0, keepdims=True))
        corr = jnp.exp(m_prev - m_new)
        p = jnp.exp(s - m_new)
        l_sc[...] = corr * l_sc[...] + jnp.sum(p, axis=0, keepdims=True)
        acc_sc[...] = acc_sc[...] * corr + _dot_tn(k, p.astype(BF16))
        m_sc[...] = m_new

    @pl.when(kj < last)
    def _():
        step(False)

    @pl.when(kj == last)
    def _():
        step(True)
        o = (acc_sc[...] / l_sc[...]).T
        o_ref[...] = o.reshape(n_heads, bq, c).astype(o_ref.dtype)


def _prompt_attention(ql, qr, k16, kr16, batch, seq, scale, bq_pref, bk_pref):
    n_heads, t, c = ql.shape
    dr = qr.shape[-1]
    bq = _pick(seq, bq_pref, 16)
    bk = _pick(seq, bk_pref, bq)
    assert bk % bq == 0
    nq, nk = seq // bq, seq // bk
    rows = n_heads * bq

    def kv_map(b, qi, kj):
        return (b * nk + jnp.minimum(kj, ((qi + 1) * bq - 1) // bk), 0)

    blocks = [((n_heads, bq, c), BF16), ((n_heads, bq, LANES_V7X), BF16), ((bk, c), BF16),
              ((bk, LANES_V7X), BF16), ((n_heads, bq, c), BF16)]
    scratch = [((1, rows), F32), ((1, rows), F32), ((c, rows), F32)]
    return pl.pallas_call(
        functools.partial(_pattn_kernel, bq=bq, bk=bk, scale=scale),
        out_shape=jax.ShapeDtypeStruct((n_heads, t, c), BF16),
        grid=(batch, nq, nk),
        in_specs=[pl.BlockSpec((n_heads, bq, c), lambda b, qi, kj: (0, b * nq + qi, 0)),
                  pl.BlockSpec((n_heads, bq, dr), lambda b, qi, kj: (0, b * nq + qi, 0)),
                  pl.BlockSpec((bk, c), kv_map),
                  pl.BlockSpec((bk, dr), kv_map)],
        out_specs=pl.BlockSpec((n_heads, bq, c), lambda b, qi, kj: (0, b * nq + qi, 0)),
        scratch_shapes=[pltpu.VMEM(s, d) for s, d in scratch],
        compiler_params=_params(("parallel", "parallel", "arbitrary"), _vmem_limit(
            blocks, scratch, temps=[((rows, bk), F32)] * 3 + [((rows, c), F32)] * 2)),
        name="prompt_attn",
    )(ql, qr, k16, kr16)


def _dot_tn(a, b):
    return lax.dot_general(a, b, (((0,), (0,)), ((), ())), preferred_element_type=F32)


def _col_scale(x, v):
    r = x.shape[0]
    vt = jnp.broadcast_to(v, (r, r)).T
    return jnp.concatenate([x[:, k * r:(k + 1) * r] * vt for k in range(x.shape[1] // r)], axis=1)


def _sattn_kernel(pt_ref, ql_ref, qr_ref, cn_ref, kn_ref, ck_hbm, kt_hbm, o_ref, kbuf, ktbuf, sem,
                  qt_sc, qrt_sc, nk_sc, nkr_sc, m_sc, l_sc, acc_sc, *, pages, layer, scale):
    n_steps = pl.num_programs(1)
    j = pl.program_id(1)
    g = pl.program_id(0) * n_steps + j
    total = pl.num_programs(0) * n_steps
    n_heads, tn, c = ql_ref.shape
    rows = n_heads * tn
    dr = qr_ref.shape[-1]
    page = nk_sc.shape[0]

    def page_copies(step, slot):
        cps = []
        for r in range(pages):
            pg = pt_ref[step * pages + r]
            cps.append(pltpu.make_async_copy(
                ck_hbm.at[layer, pg], kbuf.at[slot, pl.ds(r * page, page), :], sem.at[slot]))
            cps.append(pltpu.make_async_copy(
                kt_hbm.at[layer, pg], ktbuf.at[slot, :, pl.ds(r * page, page)], sem.at[slot]))
        return cps

    @pl.when(g == 0)
    def _():
        for cp in page_copies(0, 0):
            cp.start()

    @pl.when(j == 0)
    def _():
        qt_sc[...] = ql_ref[...].reshape(rows, c).T
        qr = qr_ref[...].reshape(rows, dr)
        qrt_sc[...] = jnp.concatenate([qr, jnp.zeros((rows, rows - dr), F32)], axis=1).T[:dr, :]
        nk_sc[...] = jnp.zeros(nk_sc.shape, F32)
        nk_sc[0:tn, :] = cn_ref[...]
        nkr_sc[...] = jnp.zeros(nkr_sc.shape, F32)
        nkr_sc[0:tn, :] = kn_ref[...]
        nk = nk_sc[...]
        s = (_dot(nk, qt_sc[...]) + _dot(nkr_sc[...], qrt_sc[...])) * scale
        u = lax.broadcasted_iota(jnp.int32, (page, rows), 0)
        t = lax.broadcasted_iota(jnp.int32, (page, rows), 1) % tn
        s = jnp.where(u <= t, s, -jnp.inf)
        m = jnp.max(s, axis=0, keepdims=True)
        p = jnp.exp(s - m)
        m_sc[...] = m
        l_sc[...] = jnp.sum(p, axis=0, keepdims=True)
        acc_sc[...] = _dot_tn(p, nk)

    def attend(slot):
        @pl.when(g + 1 < total)
        def _():
            for cp in page_copies(g + 1, 1 - slot):
                cp.start()

        for cp in page_copies(g, slot):
            cp.wait()
        k = kbuf[slot]
        s = (_dot(k, qt_sc[...]) + _dot_tn(ktbuf[slot], qrt_sc[...])) * scale
        m_prev = m_sc[...]
        m_new = jnp.maximum(m_prev, jnp.max(s, axis=0, keepdims=True))
        corr = jnp.exp(m_prev - m_new)
        p = jnp.exp(s - m_new)
        l_new = corr * l_sc[...] + jnp.sum(p, axis=0, keepdims=True)
        acc = _col_scale(acc_sc[...], corr) + _dot_tn(p, k)
        m_sc[...] = m_new
        l_sc[...] = l_new
        acc_sc[...] = acc

        @pl.when(j == n_steps - 1)
        def _():
            o_ref[...] = _col_scale(acc, 1.0 / l_new).reshape(n_heads, tn, c)

    for slot in range(2):
        pl.when(g % 2 == slot)(functools.partial(attend, slot))


def _sample_attention(ql, qr, ckv_new, kr_new, cache_ckv, cache_krt, layer, page_table, scale,
                      pages_pref):
    n_heads, t, c = ql.shape
    dr = qr.shape[-1]
    db, n_pages = page_table.shape
    tn = t // db
    page = cache_ckv.shape[2]
    pages = _pick(n_pages, pages_pref, 1)
    rows = n_heads * tn
    assert rows == LANES_V7X and tn <= page and dr <= LANES_V7X

    in_specs = [pl.BlockSpec((n_heads, tn, c), lambda b, j, pt: (0, b, 0)),
                pl.BlockSpec((n_heads, tn, dr), lambda b, j, pt: (0, b, 0)),
                pl.BlockSpec((tn, c), lambda b, j, pt: (b, 0)),
                pl.BlockSpec((tn, dr), lambda b, j, pt: (b, 0)),
                pl.BlockSpec(memory_space=pl.ANY),
                pl.BlockSpec(memory_space=pl.ANY)]
    blocks = [((n_heads, tn, c), F32), ((n_heads, tn, LANES_V7X), F32), ((tn, c), F32),
              ((tn, LANES_V7X), F32), ((n_heads, tn, c), F32)]
    scratch = [((2, pages * page, c), F32), ((2, dr, pages * page), F32),
               ((c, rows), F32), ((dr, rows), F32), ((page, c), F32), ((page, dr), F32),
               ((1, rows), F32), ((1, rows), F32), ((rows, c), F32)]
    scratch_shapes = [pltpu.VMEM(s, d) for s, d in scratch]
    scratch_shapes.insert(2, pltpu.SemaphoreType.DMA((2,)))
    return pl.pallas_call(
        functools.partial(_sattn_kernel, pages=pages, layer=layer, scale=scale),
        out_shape=jax.ShapeDtypeStruct((n_heads, t, c), F32),
        grid_spec=pltpu.PrefetchScalarGridSpec(
            num_scalar_prefetch=1,
            grid=(db, n_pages // pages),
            in_specs=in_specs,
            out_specs=pl.BlockSpec((n_heads, tn, c), lambda b, j, pt: (0, b, 0)),
            scratch_shapes=scratch_shapes),
        compiler_params=_params(("arbitrary", "arbitrary"), _vmem_limit(
            blocks, scratch, temps=[((pages * page, rows), F32)] * 3 + [((rows, c), F32)] * 3)),
        name="sample_attn",
    )(page_table.reshape(-1), ql, qr, ckv_new, kr_new, cache_ckv, cache_krt)


def _uv_kernel(ol_ref, w_ref, o_ref):
    dv = w_ref.shape[-1]
    for h in range(ol_ref.shape[0]):
        o_ref[:, h * dv:(h + 1) * dv] = _dot(ol_ref[h].astype(BF16), w_ref[h]).astype(o_ref.dtype)


def _uv(o_lat, w_uvt, tm_pref):
    n_heads, t, c = o_lat.shape
    dv = w_uvt.shape[-1]
    tm = _pick(t, tm_pref, 16)
    blocks = [((n_heads, tm, c), o_lat.dtype), (w_uvt.shape, BF16), ((tm, n_heads * dv), BF16)]
    return pl.pallas_call(
        _uv_kernel,
        out_shape=jax.ShapeDtypeStruct((t, n_heads * dv), BF16),
        grid=(t // tm,),
        in_specs=[pl.BlockSpec((n_heads, tm, c), lambda i: (0, i, 0)),
                  pl.BlockSpec(w_uvt.shape, lambda i: (0, 0, 0))],
        out_specs=pl.BlockSpec((tm, n_heads * dv), lambda i: (i, 0)),
        compiler_params=_params(("parallel",), _vmem_limit(blocks, temps=[((tm, c), BF16)] * 2)),
        name="uv",
    )(o_lat, w_uvt)


def _gate_up_kernel(h_ref, wg_ref, wu_ref, o_ref):
    h = h_ref[...]
    g = _dot(h, wg_ref[...])
    u = _dot(h, wu_ref[...])
    o_ref[...] = (g * jax.nn.sigmoid(g) * u).astype(o_ref.dtype)


def _gate_up(h, w_gate, w_up, tm_pref, tn_pref):
    t, d = h.shape
    f = w_gate.shape[1]
    tm = _pick(t, tm_pref, 16)
    tn = _pick(f, tn_pref, LANES_V7X)
    blocks = [((tm, d), BF16), ((d, tn), BF16), ((d, tn), BF16), ((tm, tn), BF16)]
    return pl.pallas_call(
        _gate_up_kernel,
        out_shape=jax.ShapeDtypeStruct((t, f), BF16),
        grid=(t // tm, f // tn),
        in_specs=[pl.BlockSpec((tm, d), lambda i, j: (i, 0)),
                  pl.BlockSpec((d, tn), lambda i, j: (0, j)),
                  pl.BlockSpec((d, tn), lambda i, j: (0, j))],
        out_specs=pl.BlockSpec((tm, tn), lambda i, j: (i, j)),
        compiler_params=_params(("parallel", "arbitrary"),
                                _vmem_limit(blocks, temps=[((tm, tn), F32)] * 4)),
        name="gate_up",
    )(h, w_gate, w_up)


def _gate_up_cast_kernel(h_ref, wg_ref, wu_ref, o_ref, wg16_ref, wu16_ref):
    @pl.when(pl.program_id(1) == 0)
    def _():
        wg16_ref[...] = wg_ref[...].astype(BF16)
        wu16_ref[...] = wu_ref[...].astype(BF16)

    h = h_ref[...]
    g = _dot(h, wg16_ref[...])
    u = _dot(h, wu16_ref[...])
    o_ref[...] = (g * jax.nn.sigmoid(g) * u).astype(o_ref.dtype)


def _gate_up_cast(h, w_gate, w_up, tm_pref, tn_pref):
    t, d = h.shape
    f = w_gate.shape[1]
    tm = _pick(t, tm_pref, 16)
    tn = _pick(f, tn_pref, LANES_V7X)
    blocks = [((tm, d), BF16), ((d, tn), F32), ((d, tn), F32), ((tm, tn), BF16),
              ((d, tn), BF16), ((d, tn), BF16)]
    w16 = jax.ShapeDtypeStruct((d, f), BF16)
    return pl.pallas_call(
        _gate_up_cast_kernel,
        out_shape=(jax.ShapeDtypeStruct((t, f), BF16), w16, w16),
        grid=(f // tn, t // tm),
        in_specs=[pl.BlockSpec((tm, d), lambda j, i: (i, 0)),
                  pl.BlockSpec((d, tn), lambda j, i: (0, j)),
                  pl.BlockSpec((d, tn), lambda j, i: (0, j))],
        out_specs=(pl.BlockSpec((tm, tn), lambda j, i: (i, j)),
                   pl.BlockSpec((d, tn), lambda j, i: (0, j)),
                   pl.BlockSpec((d, tn), lambda j, i: (0, j))),
        compiler_params=_params(("arbitrary", "arbitrary"),
                                _vmem_limit(blocks, temps=[((tm, tn), F32)] * 4)),
        name="gate_up_cast",
    )(h, w_gate, w_up)


def _rope_tables(pos, dr):
    inv_freq = ROPE_BASE ** (-jnp.arange(0, dr, 2, dtype=F32) / dr)
    ang = pos[:, None] * inv_freq[None, :]
    cos, sin = jnp.cos(ang), jnp.sin(ang)
    return jnp.concatenate([cos, cos], axis=-1), jnp.concatenate([-sin, sin], axis=-1)


def _prep_weights(w_in, w_uq, w_uk, w_uv, w_o, w_gate, w_up, w_down, rq, c, dr):
    d = w_in.shape[0]
    n_heads, dn = w_uk.shape[1], w_uk.shape[2]
    n_mla = rq + c + dr
    pad = (-n_mla) % LANES_V7X
    w_mla = jnp.concatenate([w_in[:, :n_mla], jnp.zeros((d, pad), w_in.dtype)], axis=1).astype(BF16)
    w_cv = w_in[:, n_mla:].astype(BF16)
    w_uq3 = w_uq.reshape(rq, n_heads, dn + dr)
    w_nope = w_uq3[:, :, :dn].reshape(rq, n_heads * dn).astype(BF16)
    w_rope = w_uq3[:, :, dn:].reshape(rq, n_heads * dr).astype(BF16)
    w_ukt = jnp.transpose(w_uk, (1, 2, 0)).astype(BF16)
    w_uvt = jnp.transpose(w_uv, (1, 0, 2)).astype(BF16)
    return (w_mla, w_cv, w_nope, w_rope, w_ukt, w_uvt, w_o.astype(BF16), w_gate, w_up,
            w_down.astype(BF16))


def _block(x, pos, conv_prev, attend, is_prompt, g_attn, g_q, g_kv, w_conv, g_ffn, wts, scale):
    (w_mla, w_cv, w_nope, w_rope, w_ukt, w_uvt, w_o, w_gate, w_up, w_down) = wts
    b, t, d = x.shape
    bt = b * t
    rq, c = g_q.shape[0], g_kv.shape[0]
    dr = w_rope.shape[1] // w_ukt.shape[0]
    cw = w_conv.shape[1]
    x2 = x.reshape(bt, d)
    tm_big = 1024

    h, z_mla = _norm_proj(x2, g_attn, w_mla, 512)
    z_cv = _mm([(h, 0, d)], [(w_cv, 0, d)], None, F32, tm_big, 512, "in_proj_conv")

    cos2, sin2 = _rope_tables(pos, dr)
    cos2 = jnp.tile(cos2, (b, 1))
    sin2 = jnp.tile(sin2, (b, 1))
    ql, qr = _qpath(z_mla, g_q, w_nope, w_rope, w_ukt, cos2, sin2, BF16 if is_prompt else F32, 256)
    ckv32, ckv16, kr32, kr16 = _kvpath(z_mla, g_kv, cos2, sin2, rq, 512)

    if is_prompt:
        o_lat = _prompt_attention(ql, qr, ckv16, kr16, b, t, scale, 256, 512)
        o_conv, new_conv = _conv(z_cv.reshape(b, t, 3 * cw), conv_prev, w_conv, 1, _pick(t, 512, 8))
    else:
        o_lat = attend(ql, qr, ckv32, kr32)
        o_conv, new_conv = _conv(z_cv.reshape(b, t, 3 * cw), conv_prev, w_conv, _pick(b, 16, 1), t)
    o_attn = _uv(o_lat, w_uvt, 256)
    aw = o_attn.shape[1]
    x1, h2 = _proj_norm(o_attn, o_conv.reshape(bt, cw), w_o, x2, g_ffn, 512, 512)
    if w_gate.dtype == BF16:
        act = _gate_up(h2, w_gate, w_up, tm_big, 256)
    else:
        act, w_gate, w_up = _gate_up_cast(h2, w_gate, w_up, tm_big, 256)
    f = act.shape[1]
    x_out = _mm([(act, 0, f)], [(w_down, 0, f)], x1, F32, 512, 256, "down_proj")
    return x_out, ckv32, kr32, new_conv, (w_gate, w_up)


def kernel(x_prompt, x_sample, cache_ckv, cache_krope, state_conv, page_table, g_attn, w_in, g_q,
           w_uq, g_kv, w_uk, w_uv, w_conv, w_o, g_ffn, w_gate, w_up, w_down, g_final):
    depth = w_in.shape[0]
    b, s, d = x_prompt.shape
    db, ts, _ = x_sample.shape
    rq, c = g_q.shape[1], g_kv.shape[1]
    dn = w_uk.shape[3]
    dr = w_uq.shape[2] // w_uk.shape[2] - dn
    cw = w_conv.shape[2]
    aw = w_uv.shape[2] * w_uv.shape[3]
    assert w_o.shape[1] == aw + cw and aw % cw == 0 and rq % c == 0
    past_len = page_table.shape[1] * cache_ckv.shape[2]
    scale = float(dn + dr) ** -0.5
    pos_p = jnp.arange(s, dtype=F32)
    pos_s = past_len + jnp.arange(ts, dtype=F32)

    cache_krt = jnp.swapaxes(cache_krope, -1, -2)

    xp, xs = x_prompt, x_sample
    outs = [[] for _ in range(6)]
    for l in range(depth):
        wts = _prep_weights(w_in[l], w_uq[l], w_uk[l], w_uv[l], w_o[l], w_gate[l], w_up[l],
                            w_down[l], rq, c, dr)
        small = (g_attn[l], g_q[l], g_kv[l], w_conv[l], g_ffn[l])
        conv0 = jnp.zeros((b, w_conv.shape[1] - 1, cw), F32)
        xp, ckv_p, kr_p, cv_p, ffn16 = _block(xp, pos_p, conv0, None, True, *small, wts, scale)
        wts = wts[:7] + ffn16 + wts[9:]
        xp = xp.reshape(b, s, d)
        attend = functools.partial(_sample_attention, cache_ckv=cache_ckv, cache_krt=cache_krt, layer=l,
                                   page_table=page_table, scale=scale, pages_pref=32)
        xs, ckv_s, kr_s, cv_s, _ = _block(xs, pos_s, state_conv[l], attend, False, *small, wts, scale)
        xs = xs.reshape(db, ts, d)
        for acc, val in zip(outs, (ckv_p.reshape(b, s, c), kr_p.reshape(b, s, dr), cv_p,
                                   ckv_s.reshape(db, ts, c), kr_s.reshape(db, ts, dr), cv_s)):
            acc.append(val)
    y_p = _rmsnorm(xp.reshape(b * s, d), g_final, F32).reshape(b, s, d)
    y_s = _rmsnorm(xs.reshape(db * ts, d), g_final, F32).reshape(db, ts, d)
    return (y_p, y_s) + tuple(jnp.stack(o) for o in outs)
```

```python
import functools

import jax
import jax.numpy as jnp
from jax import lax
from jax.experimental import pallas as pl
from jax.experimental.pallas import tpu as pltpu

EPS = 1e-6
ROPE_BASE = 10000.0

F32 = jnp.float32
BF16 = jnp.bfloat16

LANES_V7X = 128
SUBLANES_V7X = 8
VMEM_BYTES_V7X = 64 * 1024 * 1024
VMEM_REQUEST_CAP = VMEM_BYTES_V7X - 6 * 1024 * 1024


def _nbytes(shape, dtype):
    n = 1
    for s in shape:
        n *= s
    return n * jnp.dtype(dtype).itemsize


def _vmem_limit(blocks, scratch=(), temps=()):
    need = 2 * sum(_nbytes(s, d) for s, d in blocks)
    need += sum(_nbytes(s, d) for s, d in scratch)
    need += sum(_nbytes(s, d) for s, d in temps)
    need += 2 * 1024 * 1024
    return int(min(max(need, 16 * 1024 * 1024), VMEM_REQUEST_CAP))


def _pick(total, pref, multiple):
    best = None
    d = multiple
    while d <= min(total, pref):
        if total % d == 0:
            best = d
        d += multiple
    return best if best is not None else total


def _params(semantics, vmem):
    return pltpu.CompilerParams(dimension_semantics=semantics, vmem_limit_bytes=vmem)


def _dot(a, b):
    return jnp.dot(a, b, preferred_element_type=F32)


def _dot_nt(a, b):
    return lax.dot_general(a, b, (((1,), (1,)), ((), ())), preferred_element_type=F32)


def _rms(x, g):
    return x * lax.rsqrt(jnp.mean(x * x, axis=-1, keepdims=True) + EPS) * g


def _rope(x, cos2, sin2):
    half = x.shape[-1] // 2
    swapped = jnp.concatenate([x[:, half:], x[:, :half]], axis=-1)
    return x * cos2 + swapped * sin2


def _rmsnorm_kernel(x_ref, g_ref, o_ref):
    o_ref[...] = _rms(x_ref[...].astype(F32), g_ref[...]).astype(o_ref.dtype)


def _rmsnorm(x, g, out_dtype):
    t, d = x.shape
    tr = _pick(t, 256, SUBLANES_V7X)
    return pl.pallas_call(
        _rmsnorm_kernel,
        out_shape=jax.ShapeDtypeStruct((t, d), out_dtype),
        grid=(t // tr,),
        in_specs=[pl.BlockSpec((tr, d), lambda i: (i, 0)),
                  pl.BlockSpec((1, d), lambda i: (0, 0))],
        out_specs=pl.BlockSpec((tr, d), lambda i: (i, 0)),
        compiler_params=_params(("parallel",), _vmem_limit(
            [((tr, d), x.dtype), ((tr, d), out_dtype)], temps=[((tr, d), F32)] * 2)),
        name="rmsnorm",
    )(x, g.reshape(1, d).astype(F32))


def _mm_kernel(*refs, n_pairs, has_res):
    a_refs = refs[:n_pairs]
    w_refs = refs[n_pairs:2 * n_pairs]
    o_ref = refs[-1]
    acc = _dot(a_refs[0][...], w_refs[0][...])
    for a_ref, w_ref in zip(a_refs[1:], w_refs[1:]):
        acc = acc + _dot(a_ref[...], w_ref[...])
    if has_res:
        acc = acc + refs[2 * n_pairs][...]
    o_ref[...] = acc.astype(o_ref.dtype)


def _mm(a_list, w_list, res, out_dtype, tm_pref, tn_pref, name):
    t = a_list[0][0].shape[0]
    n = w_list[0][0].shape[1]
    tm = _pick(t, tm_pref, 16)
    tn = _pick(n, tn_pref, LANES_V7X)
    in_specs, args, blocks = [], [], []
    for a, cb, kd in a_list:
        in_specs.append(pl.BlockSpec((tm, kd), lambda i, j, cb=cb: (i, cb)))
        args.append(a)
        blocks.append(((tm, kd), a.dtype))
    for w, rb, kd in w_list:
        in_specs.append(pl.BlockSpec((kd, tn), lambda i, j, rb=rb: (rb, j)))
        args.append(w)
        blocks.append(((kd, tn), w.dtype))
    if res is not None:
        in_specs.append(pl.BlockSpec((tm, tn), lambda i, j: (i, j)))
        args.append(res)
        blocks.append(((tm, tn), res.dtype))
    blocks.append(((tm, tn), out_dtype))
    return pl.pallas_call(
        functools.partial(_mm_kernel, n_pairs=len(a_list), has_res=res is not None),
        out_shape=jax.ShapeDtypeStruct((t, n), out_dtype),
        grid=(t // tm, n // tn),
        in_specs=in_specs,
        out_specs=pl.BlockSpec((tm, tn), lambda i, j: (i, j)),
        compiler_params=_params(("parallel", "arbitrary"),
                                _vmem_limit(blocks, temps=[((tm, tn), F32)] * 2)),
        name=name,
    )(*args)


def _norm_proj_kernel(x_ref, g_ref, w_ref, h_ref, z_ref):
    h = _rms(x_ref[...], g_ref[...]).astype(h_ref.dtype)
    h_ref[...] = h
    z_ref[...] = _dot(h, w_ref[...])


def _norm_proj(x, g, w, tm_pref):
    t, d = x.shape
    n = w.shape[1]
    tm = _pick(t, tm_pref, 16)
    blocks = [((tm, d), F32), ((tm, d), BF16), ((tm, n), F32)]
    return pl.pallas_call(
        _norm_proj_kernel,
        out_shape=(jax.ShapeDtypeStruct((t, d), BF16), jax.ShapeDtypeStruct((t, n), F32)),
        grid=(t // tm,),
        in_specs=[pl.BlockSpec((tm, d), lambda i: (i, 0)),
                  pl.BlockSpec((1, d), lambda i: (0, 0)),
                  pl.BlockSpec((d, n), lambda i: (0, 0), pipeline_mode=pl.Buffered(1))],
        out_specs=(pl.BlockSpec((tm, d), lambda i: (i, 0)), pl.BlockSpec((tm, n), lambda i: (i, 0))),
        compiler_params=_params(("parallel",), _vmem_limit(
            blocks, scratch=[((d, n), BF16)], temps=[((tm, d), F32)] * 2 + [((tm, n), F32)])),
        name="norm_in_proj_mla",
    )(x, g.reshape(1, d).astype(F32), w)


def _out_proj_cast_kernel(a1_ref, a2_ref, w1_ref, w2_ref, res_ref, o_ref, w1b_ref, w2b_ref):
    w1 = w1_ref[...].astype(BF16)
    w2 = w2_ref[...].astype(BF16)
    w1b_ref[...] = w1
    w2b_ref[...] = w2
    o_ref[...] = _dot(a1_ref[...], w1) + _dot(a2_ref[...], w2) + res_ref[...]


def _out_proj_cast(a1, a2, w, res, tn_pref):
    t, k1 = a1.shape
    k2 = a2.shape[1]
    n = w.shape[1]
    assert k1 % k2 == 0 and w.shape[0] == k1 + k2
    tn = _pick(n, tn_pref, LANES_V7X)
    blocks = [((t, k1), BF16), ((t, k2), BF16), ((k1, tn), F32), ((k2, tn), F32), ((t, tn), F32),
              ((t, tn), F32), ((k1, tn), BF16), ((k2, tn), BF16)]
    return pl.pallas_call(
        _out_proj_cast_kernel,
        out_shape=(jax.ShapeDtypeStruct((t, n), F32), jax.ShapeDtypeStruct((k1, n), BF16),
                   jax.ShapeDtypeStruct((k2, n), BF16)),
        grid=(n // tn,),
        in_specs=[pl.BlockSpec((t, k1), lambda j: (0, 0)),
                  pl.BlockSpec((t, k2), lambda j: (0, 0)),
                  pl.BlockSpec((k1, tn), lambda j: (0, j)),
                  pl.BlockSpec((k2, tn), lambda j: (k1 // k2, j)),
                  pl.BlockSpec((t, tn), lambda j: (0, j))],
        out_specs=(pl.BlockSpec((t, tn), lambda j: (0, j)), pl.BlockSpec((k1, tn), lambda j: (0, j)),
                   pl.BlockSpec((k2, tn), lambda j: (0, j))),
        compiler_params=_params(("parallel",), _vmem_limit(blocks, temps=[((t, tn), F32)] * 2)),
        name="out_proj_cast",
    )(a1, a2, w, w, res)


def _qpath_kernel(cq_ref, g_ref, wn_ref, wr_ref, wuk_ref, cos_ref, sin_ref, ql_ref, qr_ref,
                  *, dn, dr):
    y = _rms(cq_ref[...], g_ref[...]).astype(BF16)
    qn = _dot(y, wn_ref[...])
    qr = _dot(y, wr_ref[...])
    cos2 = cos_ref[...]
    sin2 = sin_ref[...]
    for h in range(ql_ref.shape[0]):
        qn_h = qn[:, h * dn:(h + 1) * dn].astype(BF16)
        ql_ref[h] = _dot(qn_h, wuk_ref[h]).astype(ql_ref.dtype)
        qr_ref[h] = _rope(qr[:, h * dr:(h + 1) * dr], cos2, sin2).astype(qr_ref.dtype)


def _qpath(z_mla, g_q, w_nope, w_rope, w_ukt, cos2, sin2, out_dtype, tm_pref):
    t = z_mla.shape[0]
    rq = g_q.shape[0]
    n_heads, dn, c = w_ukt.shape
    dr = w_rope.shape[1] // n_heads
    tm = _pick(t, tm_pref, 16)
    blocks = [((tm, rq), F32), (w_nope.shape, BF16), (w_rope.shape, BF16), (w_ukt.shape, BF16),
              ((tm, LANES_V7X), F32), ((tm, LANES_V7X), F32),
              ((n_heads, tm, c), out_dtype), ((n_heads, tm, LANES_V7X), out_dtype)]
    return pl.pallas_call(
        functools.partial(_qpath_kernel, dn=dn, dr=dr),
        out_shape=(jax.ShapeDtypeStruct((n_heads, t, c), out_dtype),
                   jax.ShapeDtypeStruct((n_heads, t, dr), out_dtype)),
        grid=(t // tm,),
        in_specs=[pl.BlockSpec((tm, rq), lambda i: (i, 0)),
                  pl.BlockSpec((1, rq), lambda i: (0, 0)),
                  pl.BlockSpec(w_nope.shape, lambda i: (0, 0)),
                  pl.BlockSpec(w_rope.shape, lambda i: (0, 0)),
                  pl.BlockSpec(w_ukt.shape, lambda i: (0, 0, 0)),
                  pl.BlockSpec((tm, dr), lambda i: (i, 0)),
                  pl.BlockSpec((tm, dr), lambda i: (i, 0))],
        out_specs=(pl.BlockSpec((n_heads, tm, c), lambda i: (0, i, 0)),
                   pl.BlockSpec((n_heads, tm, dr), lambda i: (0, i, 0))),
        compiler_params=_params(("parallel",), _vmem_limit(
            blocks, temps=[((tm, w_nope.shape[1]), F32), ((tm, w_rope.shape[1]), F32),
                           ((tm, c), F32)])),
        name="qpath",
    )(z_mla, g_q.reshape(1, rq).astype(F32), w_nope, w_rope, w_ukt, cos2, sin2)


def _kvpath_kernel(ckv_ref, kr_ref, g_ref, cos_ref, sin_ref, c32_ref, c16_ref, k32_ref, k16_ref):
    ckv = _rms(ckv_ref[...], g_ref[...])
    c32_ref[...] = ckv
    c16_ref[...] = ckv.astype(BF16)
    dr = k32_ref.shape[-1]
    kr = _rope(kr_ref[:, :dr], cos_ref[...], sin_ref[...])
    k32_ref[...] = kr
    k16_ref[...] = kr.astype(BF16)


def _kvpath(z_mla, g_kv, cos2, sin2, rq, tm_pref):
    t = z_mla.shape[0]
    c = g_kv.shape[0]
    dr = cos2.shape[1]
    tm = _pick(t, tm_pref, 16)
    blocks = [((tm, c), F32), ((tm, LANES_V7X), F32), ((tm, LANES_V7X), F32) , ((tm, LANES_V7X), F32),
              ((tm, c), F32), ((tm, c), BF16), ((tm, LANES_V7X), F32), ((tm, LANES_V7X), BF16)]
    return pl.pallas_call(
        _kvpath_kernel,
        out_shape=(jax.ShapeDtypeStruct((t, c), F32), jax.ShapeDtypeStruct((t, c), BF16),
                   jax.ShapeDtypeStruct((t, dr), F32), jax.ShapeDtypeStruct((t, dr), BF16)),
        grid=(t // tm,),
        in_specs=[pl.BlockSpec((tm, c), lambda i: (i, rq // c)),
                  pl.BlockSpec((tm, LANES_V7X), lambda i: (i, (rq + c) // LANES_V7X)),
                  pl.BlockSpec((1, c), lambda i: (0, 0)),
                  pl.BlockSpec((tm, dr), lambda i: (i, 0)),
                  pl.BlockSpec((tm, dr), lambda i: (i, 0))],
        out_specs=(pl.BlockSpec((tm, c), lambda i: (i, 0)), pl.BlockSpec((tm, c), lambda i: (i, 0)),
                   pl.BlockSpec((tm, dr), lambda i: (i, 0)), pl.BlockSpec((tm, dr), lambda i: (i, 0))),
        compiler_params=_params(("parallel",), _vmem_limit(blocks, temps=[((tm, c), F32)] * 2)),
        name="kvpath",
    )(z_mla, z_mla, g_kv.reshape(1, c).astype(F32), cos2, sin2)


def _conv_kernel(gb_ref, gc_ref, u_ref, gcp_ref, up_ref, cp_ref, w_ref, o_ref, nc_ref):
    i = pl.program_id(1)
    nb, tt, w = gc_ref.shape
    v = gc_ref[...] * u_ref[...]
    vp = gcp_ref[...] * up_ref[...]
    first = i == 0
    hist = cp_ref[...]
    h1 = jnp.where(first, hist[:, 1:2, :], vp[:, 7:8, :])
    h0 = jnp.where(first, hist[:, 0:1, :], vp[:, 6:7, :])
    v2 = v.reshape(nb * tt, w)
    r1 = pltpu.roll(v2, 1, axis=0).reshape(nb, tt, w)
    r2 = pltpu.roll(v2, 2, axis=0).reshape(nb, tt, w)
    tpos = lax.broadcasted_iota(jnp.int32, (nb, tt, w), 1)
    vm1 = jnp.where(tpos == 0, h1, r1)
    vm2 = jnp.where(tpos == 0, h0, jnp.where(tpos == 1, h1, r2))
    wc = w_ref[...]
    y = wc[0:1, :] * vm2 + wc[1:2, :] * vm1 + wc[2:3, :] * v
    o_ref[...] = (gb_ref[...] * y).astype(o_ref.dtype)

    @pl.when(i == pl.num_programs(1) - 1)
    def _():
        nc_ref[...] = v[:, tt - 2:tt, :]


def _conv(z_conv, conv_prev, w_conv, nb, tt):
    b, t, w3 = z_conv.shape
    w = w3 // 3
    assert w_conv.shape[0] == 3 and conv_prev.shape[1] == 2
    pb = tt // SUBLANES_V7X

    def prev_map(col):
        return lambda bi, i: (bi, jnp.maximum(i * pb - 1, 0), col)

    blocks = [((nb, tt, w), F32)] * 3 + [((nb, 8, w), F32)] * 4 + [((nb, tt, w), BF16)]
    return pl.pallas_call(
        _conv_kernel,
        out_shape=(jax.ShapeDtypeStruct((b, t, w), BF16), jax.ShapeDtypeStruct((b, 2, w), F32)),
        grid=(b // nb, t // tt),
        in_specs=[pl.BlockSpec((nb, tt, w), lambda bi, i: (bi, i, 0)),
                  pl.BlockSpec((nb, tt, w), lambda bi, i: (bi, i, 1)),
                  pl.BlockSpec((nb, tt, w), lambda bi, i: (bi, i, 2)),
                  pl.BlockSpec((nb, SUBLANES_V7X, w), prev_map(1)),
                  pl.BlockSpec((nb, SUBLANES_V7X, w), prev_map(2)),
                  pl.BlockSpec((nb, 2, w), lambda bi, i: (bi, 0, 0)),
                  pl.BlockSpec((3, w), lambda bi, i: (0, 0))],
        out_specs=(pl.BlockSpec((nb, tt, w), lambda bi, i: (bi, i, 0)),
                   pl.BlockSpec((nb, 2, w), lambda bi, i: (bi, 0, 0))),
        compiler_params=_params(("parallel", "arbitrary"),
                                _vmem_limit(blocks, temps=[((nb, tt, w), F32)] * 6)),
        name="conv",
    )(z_conv, z_conv, z_conv, z_conv, z_conv, conv_prev, w_conv)


def _pattn_kernel(ql_ref, qr_ref, k_ref, kr_ref, o_ref, m_sc, l_sc, acc_sc, *, bq, bk, scale):
    qi = pl.program_id(1)
    kj = pl.program_id(2)
    n_heads, _, c = ql_ref.shape
    cols = n_heads * bq
    last = ((qi + 1) * bq - 1) // bk

    @pl.when(kj == 0)
    def _():
        m_sc[...] = jnp.full(m_sc.shape, -jnp.inf, F32)
        l_sc[...] = jnp.zeros(l_sc.shape, F32)
        acc_sc[...] = jnp.zeros(acc_sc.shape, F32)

    def step(masked):
        q = ql_ref[...].reshape(cols, c)
        qr = qr_ref[...].reshape(cols, qr_ref.shape[-1])
        k = k_ref[...]
        s = (_dot_nt(k, q) + _dot_nt(kr_ref[...], qr)) * scale
        if masked:
            kpos = kj * bk + lax.broadcasted_iota(jnp.int32, (bk, cols), 0)
            qpos = qi * bq + lax.broadcasted_iota(jnp.int32, (bk, cols), 1) % bq
            s = jnp.where(kpos <= qpos, s, -jnp.inf)
        m_prev = m_sc[...]
        m_new = jnp.maximum(m_prev, jnp.max(s, axis=0, keepdims=True))
        corr = jnp.exp(m_prev - m_new)
        p = jnp.exp(s - m_new)
        l_sc[...] = corr * l_sc[...] + jnp.sum(p, axis=0, keepdims=True)
        acc_sc[...] = acc_sc[...] * corr + _dot_tn(k, p.astype(BF16))
        m_sc[...] = m_new

    @pl.when(kj < last)
    def _():
        step(False)

    @pl.when(kj == last)
    def _():
        step(True)
        o = (acc_sc[...] / l_sc[...]).T
        o_ref[...] = o.reshape(n_heads, bq, c).astype(o_ref.dtype)


def _prompt_attention(ql, qr, k16, kr16, batch, seq, scale, bq_pref, bk_pref):
    n_heads, t, c = ql.shape
    dr = qr.shape[-1]
    bq = _pick(seq, bq_pref, 16)
    bk = _pick(seq, bk_pref, bq)
    assert bk % bq == 0
    nq, nk = seq // bq, seq // bk
    rows = n_heads * bq

    def kv_map(b, qi, kj):
        return (b * nk + jnp.minimum(kj, ((qi + 1) * bq - 1) // bk), 0)

    blocks = [((n_heads, bq, c), BF16), ((n_heads, bq, LANES_V7X), BF16), ((bk, c), BF16),
              ((bk, LANES_V7X), BF16), ((n_heads, bq, c), BF16)]
    scratch = [((1, rows), F32), ((1, rows), F32), ((c, rows), F32)]
    return pl.pallas_call(
        functools.partial(_pattn_kernel, bq=bq, bk=bk, scale=scale),
        out_shape=jax.ShapeDtypeStruct((n_heads, t, c), BF16),
        grid=(batch, nq, nk),
        in_specs=[pl.BlockSpec((n_heads, bq, c), lambda b, qi, kj: (0, b * nq + qi, 0)),
                  pl.BlockSpec((n_heads, bq, dr), lambda b, qi, kj: (0, b * nq + qi, 0)),
                  pl.BlockSpec((bk, c), kv_map),
                  pl.BlockSpec((bk, dr), kv_map)],
        out_specs=pl.BlockSpec((n_heads, bq, c), lambda b, qi, kj: (0, b * nq + qi, 0)),
        scratch_shapes=[pltpu.VMEM(s, d) for s, d in scratch],
        compiler_params=_params(("parallel", "parallel", "arbitrary"), _vmem_limit(
            blocks, scratch, temps=[((rows, bk), F32)] * 3 + [((rows, c), F32)] * 2)),
        name="prompt_attn",
    )(ql, qr, k16, kr16)


def _dot_tn(a, b):
    return lax.dot_general(a, b, (((0,), (0,)), ((), ())), preferred_element_type=F32)


def _col_scale(x, v):
    r = x.shape[0]
    vt = jnp.broadcast_to(v, (r, r)).T
    return jnp.concatenate([x[:, k * r:(k + 1) * r] * vt for k in range(x.shape[1] // r)], axis=1)


def _sattn_kernel(pt_ref, ql_ref, qr_ref, cn_ref, kn_ref, ck_hbm, kt_hbm, o_ref, kbuf, ktbuf, sem,
                  qt_sc, qrt_sc, nk_sc, nkr_sc, m_sc, l_sc, acc_sc, *, pages, layer, scale):
    n_steps = pl.num_programs(1)
    j = pl.program_id(1)
    g = pl.program_id(0) * n_steps + j
    total = pl.num_programs(0) * n_steps
    n_heads, tn, c = ql_ref.shape
    rows = n_heads * tn
    dr = qr_ref.shape[-1]
    page = nk_sc.shape[0]

    def page_copies(step, slot):
        cps = []
        for r in range(pages):
            pg = pt_ref[step * pages + r]
            cps.append(pltpu.make_async_copy(
                ck_hbm.at[layer, pg], kbuf.at[slot, pl.ds(r * page, page), :], sem.at[slot]))
            cps.append(pltpu.make_async_copy(
                kt_hbm.at[layer, pg], ktbuf.at[slot, :, pl.ds(r * page, page)], sem.at[slot]))
        return cps

    @pl.when(g == 0)
    def _():
        for cp in page_copies(0, 0):
            cp.start()

    @pl.when(j == 0)
    def _():
        qt_sc[...] = ql_ref[...].reshape(rows, c).T
        qr = qr_ref[...].reshape(rows, dr)
        qrt_sc[...] = jnp.concatenate([qr, jnp.zeros((rows, rows - dr), F32)], axis=1).T[:dr, :]
        nk_sc[...] = jnp.zeros(nk_sc.shape, F32)
        nk_sc[0:tn, :] = cn_ref[...]
        nkr_sc[...] = jnp.zeros(nkr_sc.shape, F32)
        nkr_sc[0:tn, :] = kn_ref[...]
        nk = nk_sc[...]
        s = (_dot(nk, qt_sc[...]) + _dot(nkr_sc[...], qrt_sc[...])) * scale
        u = lax.broadcasted_iota(jnp.int32, (page, rows), 0)
        t = lax.broadcasted_iota(jnp.int32, (page, rows), 1) % tn
        s = jnp.where(u <= t, s, -jnp.inf)
        m = jnp.max(s, axis=0, keepdims=True)
        p = jnp.exp(s - m)
        m_sc[...] = m
        l_sc[...] = jnp.sum(p, axis=0, keepdims=True)
        acc_sc[...] = _dot_tn(p, nk)

    def attend(slot):
        @pl.when(g + 1 < total)
        def _():
            for cp in page_copies(g + 1, 1 - slot):
                cp.start()

        for cp in page_copies(g, slot):
            cp.wait()
        k = kbuf[slot]
        s = (_dot(k, qt_sc[...]) + _dot_tn(ktbuf[slot], qrt_sc[...])) * scale
        m_prev = m_sc[...]
        m_new = jnp.maximum(m_prev, jnp.max(s, axis=0, keepdims=True))
        corr = jnp.exp(m_prev - m_new)
        p = jnp.exp(s - m_new)
        l_new = corr * l_sc[...] + jnp.sum(p, axis=0, keepdims=True)
        acc = _col_scale(acc_sc[...], corr) + _dot_tn(p, k)
        m_sc[...] = m_new
        l_sc[...] = l_new
        acc_sc[...] = acc

        @pl.when(j == n_steps - 1)
        def _():
            o_ref[...] = _col_scale(acc, 1.0 / l_new).reshape(n_heads, tn, c)

    for slot in range(2):
        pl.when(g % 2 == slot)(functools.partial(attend, slot))


def _sample_attention(ql, qr, ckv_new, kr_new, cache_ckv, cache_krt, layer, page_table, scale,
                      pages_pref):
    n_heads, t, c = ql.shape
    dr = qr.shape[-1]
    db, n_pages = page_table.shape
    tn = t // db
    page = cache_ckv.shape[2]
    pages = _pick(n_pages, pages_pref, 1)
    rows = n_heads * tn
    assert rows == LANES_V7X and tn <= page and dr <= LANES_V7X

    in_specs = [pl.BlockSpec((n_heads, tn, c), lambda b, j, pt: (0, b, 0)),
                pl.BlockSpec((n_heads, tn, dr), lambda b, j, pt: (0, b, 0)),
                pl.BlockSpec((tn, c), lambda b, j, pt: (b, 0)),
                pl.BlockSpec((tn, dr), lambda b, j, pt: (b, 0)),
                pl.BlockSpec(memory_space=pl.ANY),
                pl.BlockSpec(memory_space=pl.ANY)]
    blocks = [((n_heads, tn, c), F32), ((n_heads, tn, LANES_V7X), F32), ((tn, c), F32),
              ((tn, LANES_V7X), F32), ((n_heads, tn, c), F32)]
    scratch = [((2, pages * page, c), F32), ((2, dr, pages * page), F32),
               ((c, rows), F32), ((dr, rows), F32), ((page, c), F32), ((page, dr), F32),
               ((1, rows), F32), ((1, rows), F32), ((rows, c), F32)]
    scratch_shapes = [pltpu.VMEM(s, d) for s, d in scratch]
    scratch_shapes.insert(2, pltpu.SemaphoreType.DMA((2,)))
    return pl.pallas_call(
        functools.partial(_sattn_kernel, pages=pages, layer=layer, scale=scale),
        out_shape=jax.ShapeDtypeStruct((n_heads, t, c), F32),
        grid_spec=pltpu.PrefetchScalarGridSpec(
            num_scalar_prefetch=1,
            grid=(db, n_pages // pages),
            in_specs=in_specs,
            out_specs=pl.BlockSpec((n_heads, tn, c), lambda b, j, pt: (0, b, 0)),
            scratch_shapes=scratch_shapes),
        compiler_params=_params(("arbitrary", "arbitrary"), _vmem_limit(
            blocks, scratch, temps=[((pages * page, rows), F32)] * 3 + [((rows, c), F32)] * 3)),
        name="sample_attn",
    )(page_table.reshape(-1), ql, qr, ckv_new, kr_new, cache_ckv, cache_krt)


def _uv_kernel(ol_ref, w_ref, o_ref):
    dv = w_ref.shape[-1]
    for h in range(ol_ref.shape[0]):
        o_ref[:, h * dv:(h + 1) * dv] = _dot(ol_ref[h].astype(BF16), w_ref[h]).astype(o_ref.dtype)


def _uv(o_lat, w_uvt, tm_pref):
    n_heads, t, c = o_lat.shape
    dv = w_uvt.shape[-1]
    tm = _pick(t, tm_pref, 16)
    blocks = [((n_heads, tm, c), o_lat.dtype), (w_uvt.shape, BF16), ((tm, n_heads * dv), BF16)]
    return pl.pallas_call(
        _uv_kernel,
        out_shape=jax.ShapeDtypeStruct((t, n_heads * dv), BF16),
        grid=(t // tm,),
        in_specs=[pl.BlockSpec((n_heads, tm, c), lambda i: (0, i, 0)),
                  pl.BlockSpec(w_uvt.shape, lambda i: (0, 0, 0))],
        out_specs=pl.BlockSpec((tm, n_heads * dv), lambda i: (i, 0)),
        compiler_params=_params(("parallel",), _vmem_limit(blocks, temps=[((tm, c), BF16)] * 2)),
        name="uv",
    )(o_lat, w_uvt)


def _gate_up_kernel(h_ref, wg_ref, wu_ref, o_ref):
    h = h_ref[...]
    g = _dot(h, wg_ref[...])
    u = _dot(h, wu_ref[...])
    o_ref[...] = (g * jax.nn.sigmoid(g) * u).astype(o_ref.dtype)


def _gate_up(h, w_gate, w_up, tm_pref, tn_pref):
    t, d = h.shape
    f = w_gate.shape[1]
    tm = _pick(t, tm_pref, 16)
    tn = _pick(f, tn_pref, LANES_V7X)
    blocks = [((tm, d), BF16), ((d, tn), BF16), ((d, tn), BF16), ((tm, tn), BF16)]
    return pl.pallas_call(
        _gate_up_kernel,
        out_shape=jax.ShapeDtypeStruct((t, f), BF16),
        grid=(t // tm, f // tn),
        in_specs=[pl.BlockSpec((tm, d), lambda i, j: (i, 0)),
                  pl.BlockSpec((d, tn), lambda i, j: (0, j)),
                  pl.BlockSpec((d, tn), lambda i, j: (0, j))],
        out_specs=pl.BlockSpec((tm, tn), lambda i, j: (i, j)),
        compiler_params=_params(("parallel", "arbitrary"),
                                _vmem_limit(blocks, temps=[((tm, tn), F32)] * 4)),
        name="gate_up",
    )(h, w_gate, w_up)


def _gate_up_cast_kernel(h_ref, wg_ref, wu_ref, o_ref, wg16_ref, wu16_ref):
    @pl.when(pl.program_id(1) == 0)
    def _():
        wg16_ref[...] = wg_ref[...].astype(BF16)
        wu16_ref[...] = wu_ref[...].astype(BF16)

    h = h_ref[...]
    g = _dot(h, wg16_ref[...])
    u = _dot(h, wu16_ref[...])
    o_ref[...] = (g * jax.nn.sigmoid(g) * u).astype(o_ref.dtype)


def _gate_up_cast(h, w_gate, w_up, tm_pref, tn_pref):
    t, d = h.shape
    f = w_gate.shape[1]
    tm = _pick(t, tm_pref, 16)
    tn = _pick(f, tn_pref, LANES_V7X)
    blocks = [((tm, d), BF16), ((d, tn), F32), ((d, tn), F32), ((tm, tn), BF16),
              ((d, tn), BF16), ((d, tn), BF16)]
    w16 = jax.ShapeDtypeStruct((d, f), BF16)
    return pl.pallas_call(
        _gate_up_cast_kernel,
        out_shape=(jax.ShapeDtypeStruct((t, f), BF16), w16, w16),
        grid=(f // tn, t // tm),
        in_specs=[pl.BlockSpec((tm, d), lambda j, i: (i, 0)),
                  pl.BlockSpec((d, tn), lambda j, i: (0, j)),
                  pl.BlockSpec((d, tn), lambda j, i: (0, j))],
        out_specs=(pl.BlockSpec((tm, tn), lambda j, i: (i, j)),
                   pl.BlockSpec((d, tn), lambda j, i: (0, j)),
                   pl.BlockSpec((d, tn), lambda j, i: (0, j))),
        compiler_params=_params(("arbitrary", "arbitrary"),
                                _vmem_limit(blocks, temps=[((tm, tn), F32)] * 4)),
        name="gate_up_cast",
    )(h, w_gate, w_up)


def _rope_tables(pos, dr):
    inv_freq = ROPE_BASE ** (-jnp.arange(0, dr, 2, dtype=F32) / dr)
    ang = pos[:, None] * inv_freq[None, :]
    cos, sin = jnp.cos(ang), jnp.sin(ang)
    return jnp.concatenate([cos, cos], axis=-1), jnp.concatenate([-sin, sin], axis=-1)


def _prep_weights(w_in, w_uq, w_uk, w_uv, w_o, w_gate, w_up, w_down, rq, c, dr):
    d = w_in.shape[0]
    n_heads, dn = w_uk.shape[1], w_uk.shape[2]
    n_mla = rq + c + dr
    pad = (-n_mla) % LANES_V7X
    w_mla = jnp.concatenate([w_in[:, :n_mla], jnp.zeros((d, pad), w_in.dtype)], axis=1).astype(BF16)
    w_cv = w_in[:, n_mla:].astype(BF16)
    w_uq3 = w_uq.reshape(rq, n_heads, dn + dr)
    w_nope = w_uq3[:, :, :dn].reshape(rq, n_heads * dn).astype(BF16)
    w_rope = w_uq3[:, :, dn:].reshape(rq, n_heads * dr).astype(BF16)
    w_ukt = jnp.transpose(w_uk, (1, 2, 0)).astype(BF16)
    w_uvt = jnp.transpose(w_uv, (1, 0, 2)).astype(BF16)
    return (w_mla, w_cv, w_nope, w_rope, w_ukt, w_uvt, w_o, w_gate, w_up, w_down.astype(BF16))


def _block(x, pos, conv_prev, attend, is_prompt, g_attn, g_q, g_kv, w_conv, g_ffn, wts, scale):
    (w_mla, w_cv, w_nope, w_rope, w_ukt, w_uvt, w_o, w_gate, w_up, w_down) = wts
    b, t, d = x.shape
    bt = b * t
    rq, c = g_q.shape[0], g_kv.shape[0]
    dr = w_rope.shape[1] // w_ukt.shape[0]
    cw = w_conv.shape[1]
    x2 = x.reshape(bt, d)
    tm_big = 1024

    h, z_mla = _norm_proj(x2, g_attn, w_mla, 512)
    z_cv = _mm([(h, 0, d)], [(w_cv, 0, d)], None, F32, tm_big, 512, "in_proj_conv")

    cos2, sin2 = _rope_tables(pos, dr)
    cos2 = jnp.tile(cos2, (b, 1))
    sin2 = jnp.tile(sin2, (b, 1))
    ql, qr = _qpath(z_mla, g_q, w_nope, w_rope, w_ukt, cos2, sin2, BF16 if is_prompt else F32, 256)
    ckv32, ckv16, kr32, kr16 = _kvpath(z_mla, g_kv, cos2, sin2, rq, 512)

    if is_prompt:
        o_lat = _prompt_attention(ql, qr, ckv16, kr16, b, t, scale, 256, 512)
        o_conv, new_conv = _conv(z_cv.reshape(b, t, 3 * cw), conv_prev, w_conv, 1, _pick(t, 512, 8))
    else:
        o_lat = attend(ql, qr, ckv32, kr32)
        o_conv, new_conv = _conv(z_cv.reshape(b, t, 3 * cw), conv_prev, w_conv, _pick(b, 16, 1), t)
    o_attn = _uv(o_lat, w_uvt, 256)
    aw = o_attn.shape[1]
    o_conv2 = o_conv.reshape(bt, cw)
    if isinstance(w_o, tuple):
        x1 = _mm([(o_attn, 0, aw), (o_conv2, 0, cw)], [(w_o[0], 0, aw), (w_o[1], 0, cw)], x2, F32,
                 tm_big, 512, "out_proj")
    else:
        x1, w_top, w_bot = _out_proj_cast(o_attn, o_conv2, w_o, x2, 256)
        w_o = (w_top, w_bot)
    h2 = _rmsnorm(x1, g_ffn, BF16)
    if w_gate.dtype == BF16:
        act = _gate_up(h2, w_gate, w_up, tm_big, 256)
    else:
        act, w_gate, w_up = _gate_up_cast(h2, w_gate, w_up, tm_big, 256)
    f = act.shape[1]
    x_out = _mm([(act, 0, f)], [(w_down, 0, f)], x1, F32, 512, 256, "down_proj")
    return x_out, ckv32, kr32, new_conv, (w_o, w_gate, w_up)


def kernel(x_prompt, x_sample, cache_ckv, cache_krope, state_conv, page_table, g_attn, w_in, g_q,
           w_uq, g_kv, w_uk, w_uv, w_conv, w_o, g_ffn, w_gate, w_up, w_down, g_final):
    depth = w_in.shape[0]
    b, s, d = x_prompt.shape
    db, ts, _ = x_sample.shape
    rq, c = g_q.shape[1], g_kv.shape[1]
    dn = w_uk.shape[3]
    dr = w_uq.shape[2] // w_uk.shape[2] - dn
    cw = w_conv.shape[2]
    aw = w_uv.shape[2] * w_uv.shape[3]
    assert w_o.shape[1] == aw + cw and aw % cw == 0 and rq % c == 0
    past_len = page_table.shape[1] * cache_ckv.shape[2]
    scale = float(dn + dr) ** -0.5
    pos_p = jnp.arange(s, dtype=F32)
    pos_s = past_len + jnp.arange(ts, dtype=F32)

    cache_krt = jnp.swapaxes(cache_krope, -1, -2)

    xp, xs = x_prompt, x_sample
    outs = [[] for _ in range(6)]
    for l in range(depth):
        wts = _prep_weights(w_in[l], w_uq[l], w_uk[l], w_uv[l], w_o[l], w_gate[l], w_up[l],
                            w_down[l], rq, c, dr)
        small = (g_attn[l], g_q[l], g_kv[l], w_conv[l], g_ffn[l])
        attend = functools.partial(_sample_attention, cache_ckv=cache_ckv, cache_krt=cache_krt, layer=l,
                                   page_table=page_table, scale=scale, pages_pref=32)
        xs, ckv_s, kr_s, cv_s, w16 = _block(xs, pos_s, state_conv[l], attend, False, *small, wts, scale)
        wts = wts[:6] + w16 + wts[9:]
        conv0 = jnp.zeros((b, w_conv.shape[1] - 1, cw), F32)
        xp, ckv_p, kr_p, cv_p, _ = _block(xp, pos_p, conv0, None, True, *small, wts, scale)
        xp = xp.reshape(b, s, d)
        xs = xs.reshape(db, ts, d)
        for acc, val in zip(outs, (ckv_p.reshape(b, s, c), kr_p.reshape(b, s, dr), cv_p,
                                   ckv_s.reshape(db, ts, c), kr_s.reshape(db, ts, dr), cv_s)):
            acc.append(val)
    y_p = _rmsnorm(xp.reshape(b * s, d), g_final, F32).reshape(b, s, d)
    y_s = _rmsnorm(xs.reshape(db * ts, d), g_final, F32).reshape(db, ts, d)
    return (y_p, y_s) + tuple(jnp.stack(o) for o in outs)
```
